```python
import math
import jax
import jax.numpy as jnp
from jax import lax
import numpy as np

D_MODEL = 1024
BATCH = 8
SEQ = 2048
DEPTH = 4

CHUNK = 64
Q_BLOCK = 128
N_MIXERS = 4
D_MIX = D_MODEL
GROUP_WIDTH = D_MIX // N_MIXERS
N_HEADS = 4
HEAD_DIM = GROUP_WIDTH // N_HEADS
D_FF = 11 * D_MODEL // 4
CONV_K = 4
SSD_GROUPS = 2
SSD_STATE = 64
ROPE_BASE = 10000.0
RET_DECAY_OFFSET = 5.0
RMS_EPS = 1e-6
N_MOD = 9
FOX_COLS = 3 * GROUP_WIDTH + N_HEADS
MLSTM_COLS = 4 * GROUP_WIDTH + 2 * N_HEADS
RET_COLS = 4 * GROUP_WIDTH
SSD_COLS = 2 * GROUP_WIDTH + 2 * SSD_GROUPS * SSD_STATE + N_HEADS
D_IN_PROJ = FOX_COLS + MLSTM_COLS + RET_COLS + SSD_COLS

kernel_name = 'hybrid_chunk_causal_encoder'


def rmsnorm(x, g):
    xf = x.astype(jnp.float32)
    y = xf * lax.rsqrt(jnp.mean(xf * xf, axis=-1, keepdims=True) + RMS_EPS)
    return (y * g.astype(jnp.float32)).astype(x.dtype)


def head_rmsnorm(y, g):
    _, H, _, d = y.shape
    y = y * lax.rsqrt(jnp.mean(y * y, axis=-1, keepdims=True) + RMS_EPS)
    return y * g.astype(jnp.float32).reshape(H, 1, d)


def modulate(h, shift, scale):
    return h * (1 + scale) + shift


def swiglu(u, w13, w2):
    a, g = jnp.split(u @ w13, 2, axis=-1)
    return (jax.nn.silu(g) * a) @ w2


def to_heads(t):
    B, S, _ = t.shape
    return t.reshape(B, S, N_HEADS, -1).transpose(0, 2, 1, 3)


def merge_heads(t):
    B, H, S, d = t.shape
    return t.transpose(0, 2, 1, 3).reshape(B, S, H * d)


def causal_dwconv(x, w, b):
    C = x.shape[-1]
    y = lax.conv_general_dilated(
        x, w[:, None, :].astype(x.dtype), window_strides=(1,),
        padding=[(CONV_K - 1, 0)], dimension_numbers=('NWC', 'WIO', 'NWC'),
        feature_group_count=C)
    return y + b


def rotary(t):
    S, d = t.shape[2], t.shape[3]
    inv = 1.0 / (ROPE_BASE ** (jnp.arange(0, d, 2, dtype=jnp.float32) / d))
    ang = jnp.arange(S, dtype=jnp.float32)[:, None] * inv[None, :]
    cos, sin = jnp.cos(ang), jnp.sin(ang)
    t1, t2 = t[..., : d // 2], t[..., d // 2:]
    return jnp.concatenate([t1 * cos - t2 * sin, t1 * sin + t2 * cos], axis=-1)


def forgetting_attention(q, k, v, log_f):
    S, d = q.shape[2], q.shape[3]
    cum = jnp.cumsum(log_f, axis=-1)
    scale = d ** -0.5
    outs = []
    for blk in range(S // Q_BLOCK):
        lo, hi = blk * Q_BLOCK, (blk + 1) * Q_BLOCK
        logits = (jnp.einsum('bhtd,bhsd->bhts', q[:, :, lo:hi], k[:, :, :hi]) * scale
                  + cum[:, :, lo:hi, None] - cum[:, :, None, :hi])
        causal = (lo + jnp.arange(Q_BLOCK))[:, None] >= jnp.arange(hi)[None, :]
        p = jax.nn.softmax(jnp.where(causal, logits, -jnp.inf), axis=-1)
        outs.append(jnp.einsum('bhts,bhsd->bhtd', p, v[:, :, :hi]))
    return jnp.concatenate(outs, axis=2)


def chunked_decay_attention(q, k, v, log_a):
    B, H, S, dk = q.shape
    dv = v.shape[-1]
    n = S // CHUNK
    qc = q.reshape(B, H, n, CHUNK, dk)
    kc = k.reshape(B, H, n, CHUNK, dk)
    vc = v.reshape(B, H, n, CHUNK, dv)
    cum = jnp.cumsum(log_a.reshape(B, H, n, CHUNK), axis=-1)
    causal = jnp.tril(jnp.ones((CHUNK, CHUNK), dtype=bool))
    decay = jnp.exp(jnp.where(causal, cum[..., :, None] - cum[..., None, :], -jnp.inf))
    scores = jnp.einsum('bhntd,bhnsd->bhnts', qc, kc) * decay
    y = jnp.einsum('bhnts,bhnse->bhnte', scores, vc)
    w_end = jnp.exp(cum[..., -1:] - cum)
    kv = jnp.einsum('bhnsd,bhns,bhnse->nbhde', kc, w_end, vc)
    a_chunk = jnp.exp(cum[..., -1]).transpose(2, 0, 1)

    def step(state, inp):
        a, kv_c = inp
        return a[..., None, None] * state + kv_c, state

    _, states = lax.scan(step, jnp.zeros((B, H, dk, dv), jnp.float32), (a_chunk, kv))
    y = y + jnp.einsum('bhntd,bhnt,nbhde->bhnte', qc, jnp.exp(cum), states)
    return y.reshape(B, H, S, dv)


def mlstm_chunkwise(q, k, v, i_pre, log_f):
    B, H, S, d = q.shape
    n = S // CHUNK
    qc = (q * d ** -0.5).reshape(B, H, n, CHUNK, d)
    kc = k.reshape(B, H, n, CHUNK, d)
    vc = v.reshape(B, H, n, CHUNK, d)
    ic = i_pre.reshape(B, H, n, CHUNK)
    b = jnp.cumsum(log_f.reshape(B, H, n, CHUNK), axis=-1)
    causal = jnp.tril(jnp.ones((CHUNK, CHUNK), dtype=bool))
    logD = jnp.where(causal, b[..., :, None] - b[..., None, :] + ic[..., None, :], -jnp.inf)
    m_intra = jnp.max(logD, axis=-1)
    e = b[..., -1:] - b + ic
    g = jnp.max(e, axis=-1)
    w = jnp.exp(e - g[..., None])
    kv_loc = jnp.einsum('bhns,bhnsd,bhnse->nbhde', w, kc, vc)
    k_loc = jnp.einsum('bhns,bhnsd->nbhd', w, kc)
    b_end = b[..., -1].transpose(2, 0, 1)
    g_t = g.transpose(2, 0, 1)

    def step(carry, inp):
        C, nv, m = carry
        bl, gl, kvl, kl = inp
        m_new = jnp.maximum(bl + m, gl)
        a_old = jnp.exp(bl + m - m_new)
        a_loc = jnp.exp(gl - m_new)
        C_new = a_old[..., None, None] * C + a_loc[..., None, None] * kvl
        n_new = a_old[..., None] * nv + a_loc[..., None] * kl
        return (C_new, n_new, m_new), (C, nv, m)

    init = (jnp.zeros((B, H, d, d), jnp.float32), jnp.zeros((B, H, d), jnp.float32),
            jnp.zeros((B, H), jnp.float32))
    _, (C_in, n_in, m_in) = lax.scan(step, init, (b_end, g_t, kv_loc, k_loc))
    inter_log = b + m_in.transpose(1, 2, 0)[..., None]
    m_t = jnp.maximum(inter_log, m_intra)
    a_inter = jnp.exp(inter_log - m_t)
    P = jnp.exp(logD - m_t[..., None]) * jnp.einsum('bhntd,bhnsd->bhnts', qc, kc)
    num = (jnp.einsum('bhnts,bhnse->bhnte', P, vc)
           + a_inter[..., None] * jnp.einsum('bhntd,nbhde->bhnte', qc, C_in))
    den = P.sum(-1) + a_inter * jnp.einsum('bhntd,nbhd->bhnt', qc, n_in)
    h = num / jnp.maximum(jnp.abs(den), jnp.exp(-m_t))[..., None]
    return h.reshape(B, H, S, d)


def hybrid_mixer(u, w_in, w_out, fox_fb, mlstm_conv_w, mlstm_conv_b, mlstm_ib, mlstm_fb,
                 mlstm_norm_g, ret_norm_g, ssd_conv_w, ssd_conv_b, ssd_dt_bias, ssd_A_log,
                 ssd_D, ssd_norm_g):
    B, S, _ = u.shape
    W, H = GROUP_WIDTH, N_HEADS
    proj = (u @ w_in).astype(jnp.float32)
    fox_p, mlstm_p, ret_p, ssd_p = jnp.split(
        proj, [FOX_COLS, FOX_COLS + MLSTM_COLS, FOX_COLS + MLSTM_COLS + RET_COLS], axis=-1)

    fq, fk, fv, ff = jnp.split(fox_p, [W, 2 * W, 3 * W], axis=-1)
    fox_logf = jax.nn.log_sigmoid(ff + fox_fb).transpose(0, 2, 1)
    y_fox = merge_heads(forgetting_attention(to_heads(fq), to_heads(fk), to_heads(fv), fox_logf))

    mqk, mv, mi, mf, mo = jnp.split(mlstm_p, [2 * W, 3 * W, 3 * W + H, 3 * W + 2 * H], axis=-1)
    mqk = jax.nn.silu(causal_dwconv(mqk, mlstm_conv_w, mlstm_conv_b))
    mq, mk = jnp.split(mqk, 2, axis=-1)
    h_m = mlstm_chunkwise(to_heads(mq), to_heads(mk), to_heads(mv),
                          (mi + mlstm_ib).transpose(0, 2, 1),
                          jax.nn.log_sigmoid(mf + mlstm_fb).transpose(0, 2, 1))
    y_mlstm = jax.nn.sigmoid(mo) * merge_heads(head_rmsnorm(h_m, mlstm_norm_g))

    rq, rk, rv, rg = jnp.split(ret_p, 4, axis=-1)
    rq = rotary(to_heads(rq))
    rk = rotary(to_heads(rk)) * HEAD_DIM ** -0.5
    log_gamma = jnp.log(1.0 - 2.0 ** (-RET_DECAY_OFFSET - jnp.arange(H, dtype=jnp.float32)))
    ret_loga = jnp.broadcast_to(log_gamma[None, :, None], (B, H, S))
    h_r = chunked_decay_attention(rq, rk, to_heads(rv), ret_loga)
    y_ret = jax.nn.silu(rg) * merge_heads(head_rmsnorm(h_r, ret_norm_g))

    sz, sxbc, sdt = jnp.split(ssd_p, [W, 2 * W + 2 * SSD_GROUPS * SSD_STATE], axis=-1)
    sxbc = jax.nn.silu(causal_dwconv(sxbc, ssd_conv_w, ssd_conv_b))
    sx, sB, sC = jnp.split(sxbc, [W, W + SSD_GROUPS * SSD_STATE], axis=-1)
    dt = jax.nn.softplus(sdt + ssd_dt_bias).transpose(0, 2, 1)
    A = -jnp.exp(ssd_A_log.astype(jnp.float32))
    rep = H // SSD_GROUPS
    Bh = jnp.repeat(sB.reshape(B, S, SSD_GROUPS, SSD_STATE), rep, axis=2).transpose(0, 2, 1, 3)
    Ch = jnp.repeat(sC.reshape(B, S, SSD_GROUPS, SSD_STATE), rep, axis=2).transpose(0, 2, 1, 3)
    xh = to_heads(sx)
    h_s = chunked_decay_attention(Ch, Bh, xh * dt[..., None], dt * A[:, None])
    h_s = h_s + ssd_D[:, None, None] * xh
    y_ssd = rmsnorm(merge_heads(h_s) * jax.nn.silu(sz), ssd_norm_g)

    y = jnp.concatenate([y_fox, y_mlstm, y_ret, y_ssd], axis=-1)
    return y.astype(u.dtype) @ w_out


def setup_inputs(seed: int = 0) -> dict:
    key = jax.random.key(seed)
    ks = jax.random.split(key, 25)
    f32 = jnp.float32
    L, D, W, H = DEPTH, D_MODEL, GROUP_WIDTH, N_HEADS

    def nrm(k, shape, s):
        return s * jax.random.normal(k, shape, f32)

    dt0 = jnp.exp(jax.random.uniform(ks[20], (L, H), f32, math.log(1e-3), math.log(1e-1)))
    return {
        'x': nrm(ks[0], (BATCH, SEQ, D), 1.0),
        'c': nrm(ks[1], (BATCH, D), 1.0),
        'ada_w': nrm(ks[2], (L, D, N_MOD * D), 0.5 * D ** -0.5),
        'ada_b': nrm(ks[3], (L, N_MOD * D), 0.02),
        'norm_g': 1.0 + nrm(ks[4], (L, 3, D), 0.02),
        'ffn1_w13': nrm(ks[5], (L, D, 2 * D_FF), D ** -0.5),
        'ffn1_w2': nrm(ks[6], (L, D_FF, D), D_FF ** -0.5),
        'ffn2_w13': nrm(ks[7], (L, D, 2 * D_FF), D ** -0.5),
        'ffn2_w2': nrm(ks[8], (L, D_FF, D), D_FF ** -0.5),
        'w_in': nrm(ks[9], (L, D, D_IN_PROJ), D ** -0.5),
        'w_out': nrm(ks[10], (L, D_MIX, D), D_MIX ** -0.5),
        'fox_fb': 3.0 + nrm(ks[11], (L, H), 0.5),
        'mlstm_conv_w': nrm(ks[12], (L, CONV_K, 2 * W), CONV_K ** -0.5),
        'mlstm_conv_b': nrm(ks[13], (L, 2 * W), 0.02),
        'mlstm_ib': nrm(ks[14], (L, H), 0.1),
        'mlstm_fb': jnp.linspace(3.0, 6.0, H, dtype=f32)[None, :] + nrm(ks[15], (L, H), 0.1),
        'mlstm_norm_g': 1.0 + nrm(ks[16], (L, W), 0.02),
        'ret_norm_g': 1.0 + nrm(ks[17], (L, W), 0.02),
        'ssd_conv_w': nrm(ks[18], (L, CONV_K, W + 2 * SSD_GROUPS * SSD_STATE), CONV_K ** -0.5),
        'ssd_conv_b': nrm(ks[19], (L, W + 2 * SSD_GROUPS * SSD_STATE), 0.02),
        'ssd_dt_bias': dt0 + jnp.log(-jnp.expm1(-dt0)),
        'ssd_A_log': jnp.log(jax.random.uniform(ks[21], (L, H), f32, 1.0, 16.0)),
        'ssd_D': 1.0 + nrm(ks[22], (L, H), 0.1),
        'ssd_norm_g': 1.0 + nrm(ks[23], (L, W), 0.02),
        'final_g': 1.0 + nrm(ks[24], (D,), 0.02),
    }


def reference(x, c, ada_w, ada_b, norm_g, ffn1_w13, ffn1_w2, ffn2_w13, ffn2_w2, w_in, w_out,
              fox_fb, mlstm_conv_w, mlstm_conv_b, mlstm_ib, mlstm_fb, mlstm_norm_g, ret_norm_g,
              ssd_conv_w, ssd_conv_b, ssd_dt_bias, ssd_A_log, ssd_D, ssd_norm_g, final_g):
    c_act = jax.nn.silu(c)
    for l in range(DEPTH):
        cond = (c_act @ ada_w[l] + ada_b[l])[:, None, :]
        sh1, sc1, g1, sh2, sc2, g2, sh3, sc3, g3 = jnp.split(cond, N_MOD, axis=-1)
        h = modulate(rmsnorm(x, norm_g[l, 0]), sh1, sc1)
        x = x + 0.5 * g1 * swiglu(h, ffn1_w13[l], ffn1_w2[l])
        h = modulate(rmsnorm(x, norm_g[l, 1]), sh2, sc2)
        x = x + g2 * hybrid_mixer(h, w_in[l], w_out[l], fox_fb[l], mlstm_conv_w[l], mlstm_conv_b[l],
                                  mlstm_ib[l], mlstm_fb[l], mlstm_norm_g[l], ret_norm_g[l],
                                  ssd_conv_w[l], ssd_conv_b[l], ssd_dt_bias[l], ssd_A_log[l],
                                  ssd_D[l], ssd_norm_g[l])
        h = modulate(rmsnorm(x, norm_g[l, 2]), sh3, sc3)
        x = x + 0.5 * g3 * swiglu(h, ffn2_w13[l], ffn2_w2[l])
    return rmsnorm(x, final_g)
```

```python
import functools

import numpy as np
import jax
import jax.numpy as jnp
from jax import lax
from jax.experimental import pallas as pl
from jax.experimental.pallas import tpu as pltpu

F32 = jnp.float32
BF16 = jnp.bfloat16

N_HEADS = 4
HEAD_DIM = 64
GROUP_WIDTH = N_HEADS * HEAD_DIM
CONV_K = 4
SSD_GROUPS = 2
N_MOD = 9
ROPE_BASE = 10000.0
RET_DECAY_OFFSET = 5.0
RMS_EPS = 1e-6
NEG_BIG = -1e30

VMEM_LIMIT_BYTES = 56 * 1024 * 1024
LANES = 128
GATE_ROWS = 16

G_FOX, G_MI, G_MF, G_DT = 0, 4, 8, 12

FFN_ROWS = 512
INPROJ_ROWS = 256
ATT_BLOCK = 256
CHUNK = 128


def _cparams(n_axes):
    return pltpu.CompilerParams(dimension_semantics=("arbitrary",) * n_axes,
                                vmem_limit_bytes=VMEM_LIMIT_BYTES)


def _sigmoid(x):
    return 1.0 / (1.0 + jnp.exp(-x))


def _silu(x):
    return x * _sigmoid(x)


def _softplus(x):
    return jnp.maximum(x, 0.0) + jnp.log1p(jnp.exp(-jnp.abs(x)))


def _log_sigmoid(x):
    return -_softplus(-x)


def _rms_mod(x, g, sh, sc):
    y = x * lax.rsqrt(jnp.mean(x * x, axis=-1, keepdims=True) + RMS_EPS) * g
    return y * (1.0 + sc) + sh


def _dot(a, b):
    return jnp.dot(a, b, preferred_element_type=F32)


def _dot_nt(a, b):
    return lax.dot_general(a, b, (((1,), (1,)), ((), ())), preferred_element_type=F32)


def _dot_tn(a, b):
    return lax.dot_general(a, b, (((0,), (0,)), ((), ())), preferred_element_type=F32)


def _split3(x):
    hi = x.astype(BF16)
    r1 = x - hi.astype(F32)
    mid = r1.astype(BF16)
    lo = (r1 - mid.astype(F32)).astype(BF16)
    return hi, mid, lo


def _cumsum_rows(tri_lower, x):
    hi, mid, lo = _split3(x)
    return _dot(tri_lower, hi) + _dot(tri_lower, mid) + _dot(tri_lower, lo)


def _cumsum_lanes(x, tri_upper):
    hi, mid, lo = _split3(x)
    return _dot(hi, tri_upper) + _dot(mid, tri_upper) + _dot(lo, tri_upper)


def _tri(n):
    r = lax.broadcasted_iota(jnp.int32, (n, n), 0)
    c = lax.broadcasted_iota(jnp.int32, (n, n), 1)
    causal = r >= c
    lower = jnp.where(causal, 1.0, 0.0).astype(BF16)
    upper = jnp.where(r <= c, 1.0, 0.0).astype(BF16)
    return causal, lower, upper


def _head_of_lane(width=GROUP_WIDTH):
    return lax.broadcasted_iota(jnp.int32, (1, width), 1) // HEAD_DIM


def _head_masks(head_lane):
    return [jnp.where(head_lane == h, 1.0, 0.0).astype(BF16) for h in range(N_HEADS)]


def _expand(vals, head_lane):
    out = vals[N_HEADS - 1]
    for h in range(N_HEADS - 2, -1, -1):
        out = jnp.where(head_lane == h, vals[h], out)
    return out


def _head_mean_sq(y, gmat):
    hi, mid, lo = _split3(y * y)
    return _dot(hi, gmat) + _dot(mid, gmat) + _dot(lo, gmat)


def _group_mean_matrix():
    r = lax.broadcasted_iota(jnp.int32, (GROUP_WIDTH, GROUP_WIDTH), 0) // HEAD_DIM
    c = lax.broadcasted_iota(jnp.int32, (GROUP_WIDTH, GROUP_WIDTH), 1) // HEAD_DIM
    return jnp.where(r == c, 1.0 / HEAD_DIM, 0.0).astype(BF16)


def _block_diag_mask(cols):
    r = lax.broadcasted_iota(jnp.int32, (GROUP_WIDTH, cols), 0) // HEAD_DIM
    c = lax.broadcasted_iota(jnp.int32, (GROUP_WIDTH, cols), 1)
    same = (c < GROUP_WIDTH) & (r == c // HEAD_DIM)
    if cols > GROUP_WIDTH:
        same = same | ((c >= GROUP_WIDTH) & (r == c - GROUP_WIDTH))
    return same


def _causal_conv_silu(x, prev_tail, w_ref, b_ref):
    n, c = x.shape
    row = lax.broadcasted_iota(jnp.int32, (n, 1), 0)
    pad = jnp.zeros((n - 8, c), F32)
    acc = x * w_ref[CONV_K - 1:CONV_K, :] + b_ref[...]
    for j in range(1, CONV_K):
        head = jnp.concatenate([pltpu.roll(prev_tail, j, 0), pad], axis=0)
        shifted = jnp.where(row < j, head, pltpu.roll(x, j, 0))
        acc = acc + shifted * w_ref[CONV_K - 1 - j:CONV_K - j, :]
    return _silu(acc)


def _adaln_body(c_ref, w_ref, b_ref, o_ref):
    c = c_ref[...]
    o_ref[0] = _dot(_silu(c).astype(BF16), w_ref[0].astype(BF16)) + b_ref[0]


def _adaln(c, ada_w, ada_b):
    n_layers, d, n_out = ada_w.shape
    b = c.shape[0]
    tn = 1152
    return pl.pallas_call(
        _adaln_body,
        grid=(n_layers, n_out // tn),
        in_specs=[pl.BlockSpec((b, d), lambda l, j: (0, 0)),
                  pl.BlockSpec((1, d, tn), lambda l, j: (l, 0, j)),
                  pl.BlockSpec((1, 1, tn), lambda l, j: (l, 0, j))],
        out_specs=pl.BlockSpec((1, b, tn), lambda l, j: (l, 0, j)),
        out_shape=jax.ShapeDtypeStruct((n_layers, b, n_out), F32),
        compiler_params=_cparams(2),
        name="adaln",
    )(c, ada_w, ada_b.reshape(n_layers, 1, n_out))


def _cond_spec(cond_row, d):
    return pl.BlockSpec((1, 1, d), lambda b, i: (cond_row(b), 0, 0))


def _ffn_body(x_ref, sh_ref, sc_ref, gt_ref, ng_ref, w1_ref, w3_ref, w2_ref, fg_ref, o_ref, *,
              apply_final):
    x = x_ref[0]
    hb = _rms_mod(x, ng_ref[...], sh_ref[0], sc_ref[0]).astype(BF16)
    a = _dot(hb, w1_ref[...])
    g = _dot(hb, w3_ref[...])
    act = (_silu(g) * a).astype(BF16)
    y = x + (0.5 * gt_ref[0]) * _dot(act, w2_ref[...])
    if apply_final:
        y = y * lax.rsqrt(jnp.mean(y * y, axis=-1, keepdims=True) + RMS_EPS) * fg_ref[...]
    o_ref[0] = y


def _ffn(x, cond, cond_base, mod0, norm_g, w13, w2, final_g, apply_final):
    b, s, d = x.shape
    ff = w2.shape[0]
    tm = FFN_ROWS
    single = pl.Buffered(1)
    return pl.pallas_call(
        functools.partial(_ffn_body, apply_final=apply_final),
        grid=(b, s // tm),
        in_specs=[pl.BlockSpec((1, tm, d), lambda bb, i: (bb, i, 0)),
                  _cond_spec(lambda bb: cond_base(bb) + mod0, d),
                  _cond_spec(lambda bb: cond_base(bb) + mod0 + 1, d),
                  _cond_spec(lambda bb: cond_base(bb) + mod0 + 2, d),
                  pl.BlockSpec((1, d), lambda bb, i: (0, 0)),
                  pl.BlockSpec((d, ff), lambda bb, i: (0, 0), pipeline_mode=single),
                  pl.BlockSpec((d, ff), lambda bb, i: (0, 1), pipeline_mode=single),
                  pl.BlockSpec((ff, d), lambda bb, i: (0, 0), pipeline_mode=single),
                  pl.BlockSpec((1, d), lambda bb, i: (0, 0))],
        out_specs=pl.BlockSpec((1, tm, d), lambda bb, i: (bb, i, 0)),
        out_shape=jax.ShapeDtypeStruct((b, s, d), F32),
        compiler_params=_cparams(2),
        name="ffn",
    )(x, cond, cond, cond, norm_g.reshape(1, d), w13, w13, w2, final_g.reshape(1, d))


N_PB = 5 * GROUP_WIDTH
N_PM = 3 * GROUP_WIDTH
N_PR = 5 * GROUP_WIDTH
N_PS = 4 * GROUP_WIDTH
N_PROJ = N_PB + N_PM + N_PR + N_PS + LANES


def _inproj_body(x_ref, sh_ref, sc_ref, ng_ref, w_ref, wg_ref,
                 pb_ref, pm_ref, pr_ref, ps_ref, gc_ref, gr_ref):
    hb = _rms_mod(x_ref[0], ng_ref[...], sh_ref[0], sc_ref[0]).astype(BF16)
    p = _dot(hb, w_ref[...])
    o = 0
    pb_ref[0] = p[:, o:o + N_PB].astype(BF16)
    o += N_PB
    pm_ref[0] = p[:, o:o + N_PM]
    o += N_PM
    pr_ref[0] = p[:, o:o + N_PR]
    o += N_PR
    ps_ref[0] = p[:, o:o + N_PS]
    o += N_PS
    gc_ref[0] = p[:, o:o + LANES]
    gr_ref[0] = _dot_nt(wg_ref[...], hb)


def _inproj(x, cond, cond_base, norm_g, w_all, wg_t):
    b, s, d = x.shape
    tm = INPROJ_ROWS
    single = pl.Buffered(1)

    def tok(width):
        return pl.BlockSpec((1, tm, width), lambda bb, i: (bb, i, 0))

    return pl.pallas_call(
        _inproj_body,
        grid=(b, s // tm),
        in_specs=[tok(d),
                  _cond_spec(lambda bb: cond_base(bb) + 3, d),
                  _cond_spec(lambda bb: cond_base(bb) + 4, d),
                  pl.BlockSpec((1, d), lambda bb, i: (0, 0)),
                  pl.BlockSpec((d, N_PROJ), lambda bb, i: (0, 0), pipeline_mode=single),
                  pl.BlockSpec((GATE_ROWS, d), lambda bb, i: (0, 0))],
        out_specs=[tok(N_PB), tok(N_PM), tok(N_PR), tok(N_PS), tok(LANES),
                   pl.BlockSpec((1, GATE_ROWS, tm), lambda bb, i: (bb, 0, i))],
        out_shape=[jax.ShapeDtypeStruct((b, s, N_PB), BF16),
                   jax.ShapeDtypeStruct((b, s, N_PM), F32),
                   jax.ShapeDtypeStruct((b, s, N_PR), F32),
                   jax.ShapeDtypeStruct((b, s, N_PS), F32),
                   jax.ShapeDtypeStruct((b, s, LANES), F32),
                   jax.ShapeDtypeStruct((b, GATE_ROWS, s), F32)],
        compiler_params=_cparams(2),
        name="inproj",
    )(x, cond, cond, norm_g.reshape(1, d), w_all, wg_t)


def _fox_body(q_ref, k_ref, v_ref, gc_ref, gr_ref, brow_ref, bcol_ref, y_ref,
              cum_c, cum_r, acc_s, m_s, l_s):
    s_len = q_ref.shape[1]
    t = ATT_BLOCK
    nblk = s_len // t
    causal, tri_l, tri_u = _tri(t)
    hmask = _head_masks(_head_of_lane())

    def cum_step(c, carry):
        car_c, car_r = carry
        r0 = pl.multiple_of(c * t, t)
        lf_c = _log_sigmoid(gc_ref[0, pl.ds(r0, t), :] + brow_ref[...])
        cc = _cumsum_rows(tri_l, lf_c) + car_c
        cum_c[pl.ds(r0, t), :] = cc
        lf_r = _log_sigmoid(gr_ref[0, :, pl.ds(r0, t)] + bcol_ref[:, 0:1])
        cr = _cumsum_lanes(lf_r, tri_u) + car_r
        cum_r[:, pl.ds(r0, t)] = cr
        return cc[t - 1:t, :], cr[:, t - 1:t]

    lax.fori_loop(0, nblk, cum_step,
                  (jnp.zeros((1, LANES), F32), jnp.zeros((GATE_ROWS, 1), F32)))

    def q_step(i, _):
        q0 = pl.multiple_of(i * t, t)
        q = q_ref[0, pl.ds(q0, t), :]
        out = jnp.zeros((t, GROUP_WIDTH), F32)
        for h in range(N_HEADS):
            qh = q * hmask[h]
            ct = cum_c[pl.ds(q0, t), G_FOX + h:G_FOX + h + 1]
            m_s[...] = jnp.full((t, 1), NEG_BIG, F32)
            l_s[...] = jnp.zeros((t, 1), F32)
            acc_s[...] = jnp.zeros((t, GROUP_WIDTH), F32)

            def kv_block(j, diagonal):
                k0 = pl.multiple_of(j * t, t)
                kb = k_ref[0, pl.ds(k0, t), :]
                vh = v_ref[0, pl.ds(k0, t), :] * hmask[h]
                cs = cum_r[G_FOX + h:G_FOX + h + 1, pl.ds(k0, t)]
                sc = _dot_nt(qh, kb) + ct - cs
                if diagonal:
                    sc = jnp.where(causal, sc, NEG_BIG)
                m_old = m_s[...]
                m_new = jnp.maximum(m_old, jnp.max(sc, axis=-1, keepdims=True))
                alpha = jnp.exp(m_old - m_new)
                p = jnp.exp(sc - m_new)
                l_s[...] = alpha * l_s[...] + jnp.sum(p, axis=-1, keepdims=True)
                acc_s[...] = alpha * acc_s[...] + _dot(p.astype(BF16), vh)
                m_s[...] = m_new

            def off_diag(j, carry):
                kv_block(j, False)
                return carry

            lax.fori_loop(0, i, off_diag, 0)
            kv_block(i, True)
            out = out + acc_s[...] / l_s[...]
        y_ref[0, pl.ds(q0, t), :] = out.astype(BF16)
        return 0

    lax.fori_loop(0, nblk, q_step, 0)


def _fox(pb, gc, gr, bias_row, bias_col):
    b, s, _ = pb.shape
    w = GROUP_WIDTH
    t = ATT_BLOCK
    return pl.pallas_call(
        _fox_body,
        grid=(b,),
        in_specs=[pl.BlockSpec((1, s, w), lambda bb: (bb, 0, 0)),
                  pl.BlockSpec((1, s, w), lambda bb: (bb, 0, 1)),
                  pl.BlockSpec((1, s, w), lambda bb: (bb, 0, 2)),
                  pl.BlockSpec((1, s, LANES), lambda bb: (bb, 0, 0)),
                  pl.BlockSpec((1, GATE_ROWS, s), lambda bb: (bb, 0, 0)),
                  pl.BlockSpec((1, LANES), lambda bb: (0, 0)),
                  pl.BlockSpec((GATE_ROWS, LANES), lambda bb: (0, 0))],
        out_specs=pl.BlockSpec((1, s, w), lambda bb: (bb, 0, 0)),
        out_shape=jax.ShapeDtypeStruct((b, s, w), BF16),
        scratch_shapes=[pltpu.VMEM((s, LANES), F32), pltpu.VMEM((GATE_ROWS, s), F32),
                        pltpu.VMEM((t, w), F32), pltpu.VMEM((t, 1), F32), pltpu.VMEM((t, 1), F32)],
        compiler_params=_cparams(1),
        name="fox",
    )(pb, pb, pb, gc, gr, bias_row, bias_col)


def _ret_body(p_ref, v_ref, lg_ref, inv_ref, sgn_ref, ng_ref, y_ref, cos_s, sin_s, state_s):
    s_len = p_ref.shape[1]
    n = CHUNK
    w = GROUP_WIDTH
    causal, _, _ = _tri(n)
    head_lane = _head_of_lane()
    hmask = _head_masks(head_lane)
    gmat = _group_mean_matrix()
    bd = _block_diag_mask(w)

    @pl.when(pl.program_id(0) == 0)
    def _():
        def tab(c, _):
            r0 = pl.multiple_of(c * n, n)
            pos = (lax.broadcasted_iota(jnp.int32, (n, 1), 0) + r0).astype(F32)
            ang = pos * inv_ref[...]
            cos_s[pl.ds(r0, n), :] = jnp.cos(ang)
            sin_s[pl.ds(r0, n), :] = jnp.sin(ang) * sgn_ref[...]
            return 0
        lax.fori_loop(0, s_len // n, tab, 0)

    tcol = (lax.broadcasted_iota(jnp.int32, (n, 1), 0) + 1).astype(F32)
    trow = (lax.broadcasted_iota(jnp.int32, (1, n), 1) + 1).astype(F32)
    lgs = [lg_ref[0:1, h:h + 1] for h in range(N_HEADS)]
    decays = [jnp.where(causal, jnp.exp(tcol * lgs[h] - trow * lgs[h]), 0.0) for h in range(N_HEADS)]
    q_scale = _expand([jnp.exp(tcol * lgs[h]) for h in range(N_HEADS)], head_lane)
    w_end = _expand([jnp.exp(n * lgs[h] - tcol * lgs[h]) for h in range(N_HEADS)], head_lane)
    a_chunk = _expand([jnp.exp(n * lgs[h]) for h in range(N_HEADS)], head_lane)

    state_s[...] = jnp.zeros((w, w), F32)

    def chunk(c, _):
        r0 = pl.multiple_of(c * n, n)
        rows = pl.ds(r0, n)
        cos = cos_s[rows, :]
        sin = sin_s[rows, :]
        q = p_ref[0, rows, 0:w] * cos + p_ref[0, rows, w:2 * w] * sin
        k = p_ref[0, rows, 2 * w:3 * w] * cos + p_ref[0, rows, 3 * w:4 * w] * sin
        v = v_ref[0, rows, :]
        qb = q.astype(BF16)
        kb = k.astype(BF16)
        y = _dot(qb, state_s[...].astype(BF16)) * q_scale
        for h in range(N_HEADS):
            sc = _dot_nt(qb * hmask[h], kb) * decays[h]
            y = y + _dot(sc.astype(BF16), v * hmask[h])
        kv = _dot_tn((k * w_end).astype(BF16), v)
        state_s[...] = state_s[...] * a_chunk + jnp.where(bd, kv, 0.0)
        yn = y * lax.rsqrt(_head_mean_sq(y, gmat) + RMS_EPS) * ng_ref[...]
        y_ref[0, rows, :] = (_silu(p_ref[0, rows, 4 * w:5 * w]) * yn).astype(BF16)
        return 0

    lax.fori_loop(0, s_len // n, chunk, 0)


def _ret(pr, pb, lg_row, inv_row, sgn_row, norm_g):
    b, s, _ = pr.shape
    w = GROUP_WIDTH
    return pl.pallas_call(
        _ret_body,
        grid=(b,),
        in_specs=[pl.BlockSpec((1, s, N_PR), lambda bb: (bb, 0, 0)),
                  pl.BlockSpec((1, s, w), lambda bb: (bb, 0, 4)),
                  pl.BlockSpec((1, LANES), lambda bb: (0, 0)),
                  pl.BlockSpec((1, w), lambda bb: (0, 0)),
                  pl.BlockSpec((1, w), lambda bb: (0, 0)),
                  pl.BlockSpec((1, w), lambda bb: (0, 0))],
        out_specs=pl.BlockSpec((1, s, w), lambda bb: (bb, 0, 0)),
        out_shape=jax.ShapeDtypeStruct((b, s, w), BF16),
        scratch_shapes=[pltpu.VMEM((s, w), F32), pltpu.VMEM((s, w), F32), pltpu.VMEM((w, w), F32)],
        compiler_params=_cparams(1),
        name="retention",
    )(pr, pb, lg_row, inv_row, sgn_row, norm_g.reshape(1, w))


def _ssd_body(p_ref, gc_ref, gr_ref, cw_ref, cb_ref, brow_ref, bcol_ref, arow_ref, acol_ref,
              dskip_ref, ng_ref, y_ref, state_s, tail_s):
    s_len = p_ref.shape[1]
    n = CHUNK
    w = GROUP_WIDTH
    causal, tri_l, tri_u = _tri(n)
    head_lane = _head_of_lane()
    hmask = _head_masks(head_lane)
    bd = _block_diag_mask(w)

    state_s[...] = jnp.zeros((w, w), F32)
    tail_s[...] = jnp.zeros((8, 3 * w), F32)

    def chunk(c, _):
        r0 = pl.multiple_of(c * n, n)
        rows = pl.ds(r0, n)
        pre = p_ref[0, rows, w:4 * w]
        xbc = _causal_conv_silu(pre, tail_s[...], cw_ref, cb_ref)
        tail_s[...] = pre[n - 8:n, :]
        x = xbc[:, 0:w]
        kb = xbc[:, w:2 * w].astype(BF16)
        qb = xbc[:, 2 * w:3 * w].astype(BF16)

        dt_c = _softplus(gc_ref[0, rows, :] + brow_ref[...])
        cum_c = _cumsum_rows(tri_l, dt_c * arow_ref[...])
        dt_r = _softplus(gr_ref[0, :, rows] + bcol_ref[:, 0:1])
        cum_r = _cumsum_lanes(dt_r * acol_ref[:, 0:1], tri_u)

        cts = [cum_c[:, G_DT + h:G_DT + h + 1] for h in range(N_HEADS)]
        v = x * _expand([dt_c[:, G_DT + h:G_DT + h + 1] for h in range(N_HEADS)], head_lane)
        vb = v.astype(BF16)

        y = _dot(qb, state_s[...].astype(BF16)) * _expand([jnp.exp(ct) for ct in cts], head_lane)
        heads_per_group = N_HEADS // SSD_GROUPS
        qk = [_dot_nt(qb * hmask[heads_per_group * g], kb) for g in range(SSD_GROUPS)]
        for h in range(N_HEADS):
            cs = cum_r[G_DT + h:G_DT + h + 1, :]
            sc = qk[h // heads_per_group] * jnp.where(causal, jnp.exp(cts[h] - cs), 0.0)
            y = y + _dot(sc.astype(BF16), vb * hmask[h])

        ends = [ct[n - 1:n, :] for ct in cts]
        w_end = _expand([jnp.exp(ends[h] - cts[h]) for h in range(N_HEADS)], head_lane)
        kv = _dot_tn((xbc[:, w:2 * w] * w_end).astype(BF16), vb)
        a_chunk = _expand([jnp.exp(e) for e in ends], head_lane)
        state_s[...] = state_s[...] * a_chunk + jnp.where(bd, kv, 0.0)

        hs = (y + dskip_ref[...] * x) * _silu(p_ref[0, rows, 0:w])
        yn = hs * lax.rsqrt(jnp.mean(hs * hs, axis=-1, keepdims=True) + RMS_EPS) * ng_ref[...]
        y_ref[0, rows, :] = yn.astype(BF16)
        return 0

    lax.fori_loop(0, s_len // n, chunk, 0)


def _ssd(ps, gc, gr, conv_w, conv_b, bias_row, bias_col, a_row, a_col, d_row, norm_g):
    b, s, _ = ps.shape
    w = GROUP_WIDTH

    def const(shape):
        return pl.BlockSpec(shape, lambda bb: (0,) * len(shape))

    return pl.pallas_call(
        _ssd_body,
        grid=(b,),
        in_specs=[pl.BlockSpec((1, s, N_PS), lambda bb: (bb, 0, 0)),
                  pl.BlockSpec((1, s, LANES), lambda bb: (bb, 0, 0)),
                  pl.BlockSpec((1, GATE_ROWS, s), lambda bb: (bb, 0, 0)),
                  const((CONV_K, 3 * w)), const((1, 3 * w)),
                  const((1, LANES)), const((GATE_ROWS, LANES)),
                  const((1, LANES)), const((GATE_ROWS, LANES)),
                  const((1, w)), const((1, w))],
        out_specs=pl.BlockSpec((1, s, w), lambda bb: (bb, 0, 0)),
        out_shape=jax.ShapeDtypeStruct((b, s, w), BF16),
        scratch_shapes=[pltpu.VMEM((w, w), F32), pltpu.VMEM((8, 3 * w), F32)],
        compiler_params=_cparams(1),
        name="ssd",
    )(ps, gc, gr, conv_w, conv_b, bias_row, bias_col, a_row, a_col, d_row, norm_g.reshape(1, w))


def _mlstm_body(p_ref, v_ref, gc_ref, gr_ref, cw_ref, cb_ref, brow_ref, bcol_ref, ng_ref, y_ref,
                state_s, m_s, tail_s):
    s_len = p_ref.shape[1]
    n = CHUNK
    w = GROUP_WIDTH
    wa = w + LANES
    causal, tri_l, tri_u = _tri(n)
    head_lane = _head_of_lane()
    hmask = _head_masks(head_lane)
    head_lane_aug = jnp.concatenate(
        [head_lane, lax.broadcasted_iota(jnp.int32, (1, LANES), 1)], axis=1)
    gmat = _group_mean_matrix()
    bd_aug = _block_diag_mask(wa)
    lane128 = lax.broadcasted_iota(jnp.int32, (1, LANES), 1)

    state_s[...] = jnp.zeros((w, wa), F32)
    m_s[...] = jnp.zeros((1, LANES), F32)
    tail_s[...] = jnp.zeros((8, 2 * w), F32)

    def chunk(c, _):
        r0 = pl.multiple_of(c * n, n)
        rows = pl.ds(r0, n)
        pre = p_ref[0, rows, 0:2 * w]
        qk_act = _causal_conv_silu(pre, tail_s[...], cw_ref, cb_ref)
        tail_s[...] = pre[n - 8:n, :]
        qb = (qk_act[:, 0:w] * (HEAD_DIM ** -0.5)).astype(BF16)
        k = qk_act[:, w:2 * w]
        kb = k.astype(BF16)
        vb = v_ref[0, rows, :]

        g_c = gc_ref[0, rows, :] + brow_ref[...]
        b_c = _cumsum_rows(tri_l, _log_sigmoid(g_c))
        g_r = gr_ref[0, :, rows] + bcol_ref[:, 0:1]
        b_r = _cumsum_lanes(_log_sigmoid(g_r), tri_u)

        qs = _dot(qb, state_s[...].astype(BF16))
        m_row = m_s[...]
        num = jnp.zeros((n, w), F32)
        a_inters, dens, a_olds, a_locs, w_cols = [], [], [], [], []
        m_new_row = m_row
        for h in range(N_HEADS):
            bt = b_c[:, G_MF + h:G_MF + h + 1]
            bs = b_r[G_MF + h:G_MF + h + 1, :]
            i_t = g_c[:, G_MI + h:G_MI + h + 1]
            i_s = g_r[G_MI + h:G_MI + h + 1, :]
            m_in = m_row[:, h:h + 1]
            log_d = jnp.where(causal, bt - bs + i_s, NEG_BIG)
            m_intra = jnp.max(log_d, axis=-1, keepdims=True)
            inter_log = bt + m_in
            m_t = jnp.maximum(inter_log, m_intra)
            a_inter = jnp.exp(inter_log - m_t)
            p = jnp.exp(log_d - m_t) * _dot_nt(qb * hmask[h], kb)
            num = num + _dot(p.astype(BF16), vb * hmask[h])
            den = jnp.sum(p, axis=-1, keepdims=True) + a_inter * qs[:, w + h:w + h + 1]
            a_inters.append(a_inter)
            dens.append(jnp.maximum(jnp.abs(den), jnp.exp(-m_t)))
            b_end = bt[n - 1:n, :]
            g_max = jnp.max(b_end - bs + i_s, axis=-1, keepdims=True)
            w_cols.append(jnp.exp(b_end - bt + i_t - g_max))
            m_new = jnp.maximum(b_end + m_in, g_max)
            a_olds.append(jnp.exp(b_end + m_in - m_new))
            a_locs.append(jnp.exp(g_max - m_new))
            m_new_row = jnp.where(lane128 == h, m_new, m_new_row)

        num = num + _expand(a_inters, head_lane) * qs[:, 0:w]
        hval = num / _expand(dens, head_lane)

        kw = (k * _expand(w_cols, head_lane)).astype(BF16)
        v_aug = jnp.concatenate([vb, jnp.ones((n, LANES), BF16)], axis=1)
        kv = jnp.where(bd_aug, _dot_tn(kw, v_aug), 0.0)
        state_s[...] = (state_s[...] * _expand(a_olds, head_lane_aug)
                        + kv * _expand(a_locs, head_lane_aug))
        m_s[...] = m_new_row

        yn = hval * lax.rsqrt(_head_mean_sq(hval, gmat) + RMS_EPS) * ng_ref[...]
        y_ref[0, rows, :] = (_sigmoid(p_ref[0, rows, 2 * w:3 * w]) * yn).astype(BF16)
        return 0

    lax.fori_loop(0, s_len // n, chunk, 0)


def _mlstm(pm, pb, gc, gr, conv_w, conv_b, bias_row, bias_col, norm_g):
    b, s, _ = pm.shape
    w = GROUP_WIDTH

    def const(shape):
        return pl.BlockSpec(shape, lambda bb: (0,) * len(shape))

    return pl.pallas_call(
        _mlstm_body,
        grid=(b,),
        in_specs=[pl.BlockSpec((1, s, N_PM), lambda bb: (bb, 0, 0)),
                  pl.BlockSpec((1, s, w), lambda bb: (bb, 0, 3)),
                  pl.BlockSpec((1, s, LANES), lambda bb: (bb, 0, 0)),
                  pl.BlockSpec((1, GATE_ROWS, s), lambda bb: (bb, 0, 0)),
                  const((CONV_K, 2 * w)), const((1, 2 * w)),
                  const((1, LANES)), const((GATE_ROWS, LANES)), const((1, w))],
        out_specs=pl.BlockSpec((1, s, w), lambda bb: (bb, 0, 0)),
        out_shape=jax.ShapeDtypeStruct((b, s, w), BF16),
        scratch_shapes=[pltpu.VMEM((w, w + LANES), F32), pltpu.VMEM((1, LANES), F32),
                        pltpu.VMEM((8, 2 * w), F32)],
        compiler_params=_cparams(1),
        name="mlstm",
    )(pm, pb, gc, gr, conv_w, conv_b, bias_row, bias_col, norm_g.reshape(1, w))


def _outproj_body(x_ref, gt_ref, y0_ref, y1_ref, y2_ref, y3_ref, w_ref, o_ref):
    w = GROUP_WIDTH
    acc = _dot(y0_ref[0], w_ref[0:w, :])
    acc += _dot(y1_ref[0], w_ref[w:2 * w, :])
    acc += _dot(y2_ref[0], w_ref[2 * w:3 * w, :])
    acc += _dot(y3_ref[0], w_ref[3 * w:4 * w, :])
    o_ref[0] = x_ref[0] + gt_ref[0] * acc


def _outproj(x, cond, cond_base, ys, w_out):
    b, s, d = x.shape
    w = GROUP_WIDTH
    tm = FFN_ROWS
    tok = pl.BlockSpec((1, tm, d), lambda bb, i: (bb, i, 0))
    part = pl.BlockSpec((1, tm, w), lambda bb, i: (bb, i, 0))
    return pl.pallas_call(
        _outproj_body,
        grid=(b, s // tm),
        in_specs=[tok, _cond_spec(lambda bb: cond_base(bb) + 5, d), part, part, part, part,
                  pl.BlockSpec((N_HEADS * w, d), lambda bb, i: (0, 0))],
        out_specs=tok,
        out_shape=jax.ShapeDtypeStruct((b, s, d), F32),
        compiler_params=_cparams(2),
        name="outproj",
    )(x, cond, *ys, w_out)


def _inproj_columns():
    w, h, d = GROUP_WIDTH, N_HEADS, HEAD_DIM
    fox0 = 0
    ml0 = fox0 + 3 * w + h
    ret0 = ml0 + 4 * w + 2 * h
    ssd0 = ret0 + 4 * w
    end = ssd0 + 2 * w + 2 * SSD_GROUPS * d + h
    ar = np.arange
    swap = np.concatenate([hh * d + (ar(d) + d // 2) % d for hh in range(h)])
    rep = np.concatenate([(hh // (h // SSD_GROUPS)) * d + ar(d) for hh in range(h)])
    pad = end
    cols = np.concatenate([
        fox0 + ar(3 * w),
        ml0 + 2 * w + ar(w),
        ret0 + 2 * w + ar(w),
        ml0 + ar(2 * w),
        ml0 + 3 * w + 2 * h + ar(w),
        ret0 + ar(w), ret0 + swap,
        ret0 + w + ar(w), ret0 + w + swap,
        ret0 + 3 * w + ar(w),
        ssd0 + ar(w),
        ssd0 + w + ar(w),
        ssd0 + 2 * w + rep,
        ssd0 + 2 * w + SSD_GROUPS * d + rep,
    ])
    gates = np.concatenate([fox0 + 3 * w + ar(h), ml0 + 3 * w + ar(h), ml0 + 3 * w + h + ar(h),
                            ssd0 + 2 * w + 2 * SSD_GROUPS * d + ar(h)])
    gate_cols = np.concatenate([gates, np.full(LANES - gates.size, pad)])
    scale = np.ones(cols.size + LANES, np.float32)
    scale[0:w] = d ** -0.5
    k0 = N_PB + N_PM + 2 * w
    scale[k0:k0 + 2 * w] = d ** -0.5
    conv_rep = np.concatenate([ar(w), w + rep, w + SSD_GROUPS * d + rep])
    return np.concatenate([cols, gate_cols]), scale, gates, conv_rep


_COLS, _COL_SCALE, _GATE_COLS, _SSD_CONV_COLS = _inproj_columns()


def _lane_pad(v, width=LANES):
    return jnp.zeros((width,), F32).at[:v.shape[0]].set(v)


def kernel(x, c, ada_w, ada_b, norm_g, ffn1_w13, ffn1_w2, ffn2_w13, ffn2_w2, w_in, w_out, fox_fb,
           mlstm_conv_w, mlstm_conv_b, mlstm_ib, mlstm_fb, mlstm_norm_g, ret_norm_g, ssd_conv_w,
           ssd_conv_b, ssd_dt_bias, ssd_A_log, ssd_D, ssd_norm_g, final_g):
    n_layers = ada_w.shape[0]
    b, s, d = x.shape
    hd = HEAD_DIM

    cond = _adaln(c, ada_w, ada_b).reshape(n_layers * b * N_MOD, 1, d)

    inv = 1.0 / (ROPE_BASE ** (jnp.arange(0, hd, 2, dtype=F32) / hd))
    inv_row = jnp.tile(inv, 2 * N_HEADS).reshape(1, GROUP_WIDTH)
    sgn_row = jnp.tile(jnp.concatenate([-jnp.ones(hd // 2, F32), jnp.ones(hd // 2, F32)]),
                       N_HEADS).reshape(1, GROUP_WIDTH)
    log_gamma = jnp.log(1.0 - 2.0 ** (-RET_DECAY_OFFSET - jnp.arange(N_HEADS, dtype=F32)))
    lg_row = _lane_pad(log_gamma).reshape(1, LANES)

    for l in range(n_layers):
        def cond_base(bb, l=l):
            return (l * b + bb) * N_MOD

        x = _ffn(x, cond, cond_base, 0, norm_g[l, 0], ffn1_w13[l].astype(BF16),
                 ffn1_w2[l].astype(BF16), final_g, False)

        w_ext = jnp.concatenate([w_in[l], jnp.zeros((d, 1), F32)], axis=1)
        w_all = (w_ext[:, _COLS] * _COL_SCALE).astype(BF16)
        wg_t = jnp.zeros((GATE_ROWS, d), F32).at[:_GATE_COLS.size].set(w_in[l][:, _GATE_COLS].T)
        pb, pm, pr, ps, gc, gr = _inproj(x, cond, cond_base, norm_g[l, 1], w_all, wg_t.astype(BF16))

        gate_bias = _lane_pad(jnp.concatenate([fox_fb[l], mlstm_ib[l], mlstm_fb[l], ssd_dt_bias[l]]))
        bias_row = gate_bias.reshape(1, LANES)
        bias_col = jnp.broadcast_to(gate_bias[:GATE_ROWS, None], (GATE_ROWS, LANES))
        a_vec = _lane_pad(jnp.zeros((G_DT,), F32)).at[G_DT:G_DT + N_HEADS].set(
            -jnp.exp(ssd_A_log[l].astype(F32)))
        a_row = a_vec.reshape(1, LANES)
        a_col = jnp.broadcast_to(a_vec[:GATE_ROWS, None], (GATE_ROWS, LANES))
        d_row = jnp.repeat(ssd_D[l], hd).reshape(1, GROUP_WIDTH)

        y_fox = _fox(pb, gc, gr, bias_row, bias_col)
        y_mlstm = _mlstm(pm, pb, gc, gr, mlstm_conv_w[l], mlstm_conv_b[l].reshape(1, -1),
                         bias_row, bias_col, mlstm_norm_g[l])
        y_ret = _ret(pr, pb, lg_row, inv_row, sgn_row, ret_norm_g[l])
        y_ssd = _ssd(ps, gc, gr, ssd_conv_w[l][:, _SSD_CONV_COLS],
                     ssd_conv_b[l][_SSD_CONV_COLS].reshape(1, -1), bias_row, bias_col,
                     a_row, a_col, d_row, ssd_norm_g[l])

        x = _outproj(x, cond, cond_base, (y_fox, y_mlstm, y_ret, y_ssd), w_out[l].astype(BF16))
        x = _ffn(x, cond, cond_base, 6, norm_g[l, 2], ffn2_w13[l].astype(BF16),
                 ffn2_w2[l].astype(BF16), final_g, l == n_layers - 1)
    return x
```

```python
import functools

import numpy as np
import jax
import jax.numpy as jnp
from jax import lax
from jax.experimental import pallas as pl
from jax.experimental.pallas import tpu as pltpu

F32 = jnp.float32
BF16 = jnp.bfloat16

N_HEADS = 4
HEAD_DIM = 64
GROUP_WIDTH = N_HEADS * HEAD_DIM
CONV_K = 4
SSD_GROUPS = 2
N_MOD = 9
ROPE_BASE = 10000.0
RET_DECAY_OFFSET = 5.0
RMS_EPS = 1e-6
NEG_BIG = -1e30

VMEM_LIMIT_BYTES = 56 * 1024 * 1024
LANES = 128
GATE_ROWS = 16

G_FOX, G_MI, G_MF, G_DT = 0, 4, 8, 12

FFN_ROWS = 512
INPROJ_ROWS = 256
ATT_BLOCK = 256
CHUNK = 128


def _cparams(n_axes):
    return pltpu.CompilerParams(dimension_semantics=("arbitrary",) * n_axes,
                                vmem_limit_bytes=VMEM_LIMIT_BYTES)


def _sigmoid(x):
    return 1.0 / (1.0 + jnp.exp(-x))


def _silu(x):
    return x * _sigmoid(x)


def _softplus(x):
    return jnp.maximum(x, 0.0) + jnp.log1p(jnp.exp(-jnp.abs(x)))


def _log_sigmoid(x):
    return -_softplus(-x)


def _rms_mod(x, g, sh, sc):
    y = x * lax.rsqrt(jnp.mean(x * x, axis=-1, keepdims=True) + RMS_EPS) * g
    return y * (1.0 + sc) + sh


def _dot(a, b):
    return jnp.dot(a, b, preferred_element_type=F32)


def _dot_nt(a, b):
    return lax.dot_general(a, b, (((1,), (1,)), ((), ())), preferred_element_type=F32)


def _dot_tn(a, b):
    return lax.dot_general(a, b, (((0,), (0,)), ((), ())), preferred_element_type=F32)


def _split3(x):
    hi = x.astype(BF16)
    r1 = x - hi.astype(F32)
    mid = r1.astype(BF16)
    lo = (r1 - mid.astype(F32)).astype(BF16)
    return hi, mid, lo


def _cumsum_rows(tri_lower, x):
    hi, mid, lo = _split3(x)
    return _dot(tri_lower, hi) + _dot(tri_lower, mid) + _dot(tri_lower, lo)


def _cumsum_lanes(x, tri_upper):
    hi, mid, lo = _split3(x)
    return _dot(hi, tri_upper) + _dot(mid, tri_upper) + _dot(lo, tri_upper)


def _tri(n):
    r = lax.broadcasted_iota(jnp.int32, (n, n), 0)
    c = lax.broadcasted_iota(jnp.int32, (n, n), 1)
    causal = r >= c
    lower = jnp.where(causal, 1.0, 0.0).astype(BF16)
    upper = jnp.where(r <= c, 1.0, 0.0).astype(BF16)
    return causal, lower, upper


def _head_of_lane(width=GROUP_WIDTH):
    return lax.broadcasted_iota(jnp.int32, (1, width), 1) // HEAD_DIM


def _head_masks(head_lane):
    return [jnp.where(head_lane == h, 1.0, 0.0).astype(BF16) for h in range(N_HEADS)]


def _expand(vals, head_lane):
    out = vals[N_HEADS - 1]
    for h in range(N_HEADS - 2, -1, -1):
        out = jnp.where(head_lane == h, vals[h], out)
    return out


def _head_mean_sq(y, gmat):
    hi, mid, lo = _split3(y * y)
    return _dot(hi, gmat) + _dot(mid, gmat) + _dot(lo, gmat)


def _group_mean_matrix():
    r = lax.broadcasted_iota(jnp.int32, (GROUP_WIDTH, GROUP_WIDTH), 0) // HEAD_DIM
    c = lax.broadcasted_iota(jnp.int32, (GROUP_WIDTH, GROUP_WIDTH), 1) // HEAD_DIM
    return jnp.where(r == c, 1.0 / HEAD_DIM, 0.0).astype(BF16)


def _block_diag_mask(cols):
    r = lax.broadcasted_iota(jnp.int32, (GROUP_WIDTH, cols), 0) // HEAD_DIM
    c = lax.broadcasted_iota(jnp.int32, (GROUP_WIDTH, cols), 1)
    same = (c < GROUP_WIDTH) & (r == c // HEAD_DIM)
    if cols > GROUP_WIDTH:
        same = same | ((c >= GROUP_WIDTH) & (r == c - GROUP_WIDTH))
    return same


def _causal_conv_silu(x, prev_tail, w_ref, b_ref):
    n, c = x.shape
    row = lax.broadcasted_iota(jnp.int32, (n, 1), 0)
    pad = jnp.zeros((n - 8, c), F32)
    acc = x * w_ref[CONV_K - 1:CONV_K, :] + b_ref[...]
    for j in range(1, CONV_K):
        head = jnp.concatenate([pltpu.roll(prev_tail, j, 0), pad], axis=0)
        shifted = jnp.where(row < j, head, pltpu.roll(x, j, 0))
        acc = acc + shifted * w_ref[CONV_K - 1 - j:CONV_K - j, :]
    return _silu(acc)


def _adaln_body(c_ref, w_ref, b_ref, o_ref):
    c = c_ref[...]
    o_ref[0] = _dot(_silu(c).astype(BF16), w_ref[0].astype(BF16)) + b_ref[0]


def _adaln(c, ada_w, ada_b):
    n_layers, d, n_out = ada_w.shape
    b = c.shape[0]
    tn = 1152
    return pl.pallas_call(
        _adaln_body,
        grid=(n_layers, n_out // tn),
        in_specs=[pl.BlockSpec((b, d), lambda l, j: (0, 0)),
                  pl.BlockSpec((1, d, tn), lambda l, j: (l, 0, j)),
                  pl.BlockSpec((1, 1, tn), lambda l, j: (l, 0, j))],
        out_specs=pl.BlockSpec((1, b, tn), lambda l, j: (l, 0, j)),
        out_shape=jax.ShapeDtypeStruct((n_layers, b, n_out), F32),
        compiler_params=_cparams(2),
        name="adaln",
    )(c, ada_w, ada_b.reshape(n_layers, 1, n_out))


def _cond_spec(cond_row, d):
    return pl.BlockSpec((1, 1, d), lambda b, i: (cond_row(b), 0, 0))


def _ffn_body(x_ref, sh_ref, sc_ref, gt_ref, ng_ref, w1_ref, w3_ref, w2_ref, fg_ref, o_ref, *,
              apply_final):
    x = x_ref[0]
    hb = _rms_mod(x, ng_ref[...], sh_ref[0], sc_ref[0]).astype(BF16)
    a = _dot(hb, w1_ref[...])
    g = _dot(hb, w3_ref[...])
    act = (_silu(g) * a).astype(BF16)
    y = x + (0.5 * gt_ref[0]) * _dot(act, w2_ref[...])
    if apply_final:
        y = y * lax.rsqrt(jnp.mean(y * y, axis=-1, keepdims=True) + RMS_EPS) * fg_ref[...]
    o_ref[0] = y


def _ffn(x, cond, cond_base, mod0, norm_g, w13, w2, layer, final_g, apply_final):
    b, s, d = x.shape
    ff = w2.shape[1]
    tm = FFN_ROWS
    single = pl.Buffered(1)
    return pl.pallas_call(
        functools.partial(_ffn_body, apply_final=apply_final),
        grid=(b, s // tm),
        in_specs=[pl.BlockSpec((1, tm, d), lambda bb, i: (bb, i, 0)),
                  _cond_spec(lambda bb: cond_base(bb) + mod0, d),
                  _cond_spec(lambda bb: cond_base(bb) + mod0 + 1, d),
                  _cond_spec(lambda bb: cond_base(bb) + mod0 + 2, d),
                  pl.BlockSpec((1, d), lambda bb, i: (0, 0)),
                  pl.BlockSpec((None, d, ff), lambda bb, i: (layer, 0, 0), pipeline_mode=single),
                  pl.BlockSpec((None, d, ff), lambda bb, i: (layer, 0, 1), pipeline_mode=single),
                  pl.BlockSpec((None, ff, d), lambda bb, i: (layer, 0, 0), pipeline_mode=single),
                  pl.BlockSpec((1, d), lambda bb, i: (0, 0))],
        out_specs=pl.BlockSpec((1, tm, d), lambda bb, i: (bb, i, 0)),
        out_shape=jax.ShapeDtypeStruct((b, s, d), F32),
        compiler_params=_cparams(2),
        name="ffn",
    )(x, cond, cond, cond, norm_g.reshape(1, d), w13, w13, w2, final_g.reshape(1, d))


N_PB = 3 * GROUP_WIDTH
N_PM = 3 * GROUP_WIDTH
N_PR = 5 * GROUP_WIDTH
N_PS = 4 * GROUP_WIDTH
N_PROJ = N_PB + N_PM + N_PR + N_PS + LANES
N_PT = 2 * GROUP_WIDTH


def _inproj_body(x_ref, sh_ref, sc_ref, ng_ref, w_ref, wt_ref,
                 pb_ref, pm_ref, pr_ref, ps_ref, gc_ref, pt_ref, gr_ref):
    hb = _rms_mod(x_ref[0], ng_ref[...], sh_ref[0], sc_ref[0]).astype(BF16)
    p = _dot(hb, w_ref[...])
    o = 0
    pb_ref[0] = p[:, o:o + N_PB].astype(BF16)
    o += N_PB
    pm_ref[0] = p[:, o:o + N_PM]
    o += N_PM
    pr_ref[0] = p[:, o:o + N_PR]
    o += N_PR
    ps_ref[0] = p[:, o:o + N_PS]
    o += N_PS
    gc_ref[0] = p[:, o:o + LANES]
    pt = _dot_nt(wt_ref[...], hb)
    pt_ref[0] = pt[0:N_PT, :].astype(BF16)
    gr_ref[0] = pt[N_PT:N_PT + GATE_ROWS, :]


def _inproj(x, cond, cond_base, norm_g, w_all, w_t, layer):
    b, s, d = x.shape
    tm = INPROJ_ROWS
    single = pl.Buffered(1)

    def tok(width):
        return pl.BlockSpec((1, tm, width), lambda bb, i: (bb, i, 0))

    return pl.pallas_call(
        _inproj_body,
        grid=(b, s // tm),
        in_specs=[tok(d),
                  _cond_spec(lambda bb: cond_base(bb) + 3, d),
                  _cond_spec(lambda bb: cond_base(bb) + 4, d),
                  pl.BlockSpec((1, d), lambda bb, i: (0, 0)),
                  pl.BlockSpec((None, d, N_PROJ), lambda bb, i: (layer, 0, 0), pipeline_mode=single),
                  pl.BlockSpec((None, N_PT + GATE_ROWS, d), lambda bb, i: (layer, 0, 0))],
        out_specs=[tok(N_PB), tok(N_PM), tok(N_PR), tok(N_PS), tok(LANES),
                   pl.BlockSpec((1, N_PT, tm), lambda bb, i: (bb, 0, i)),
                   pl.BlockSpec((1, GATE_ROWS, tm), lambda bb, i: (bb, 0, i))],
        out_shape=[jax.ShapeDtypeStruct((b, s, N_PB), BF16),
                   jax.ShapeDtypeStruct((b, s, N_PM), F32),
                   jax.ShapeDtypeStruct((b, s, N_PR), F32),
                   jax.ShapeDtypeStruct((b, s, N_PS), F32),
                   jax.ShapeDtypeStruct((b, s, LANES), F32),
                   jax.ShapeDtypeStruct((b, N_PT, s), BF16),
                   jax.ShapeDtypeStruct((b, GATE_ROWS, s), F32)],
        compiler_params=_cparams(2),
        name="inproj",
    )(x, cond, cond, norm_g.reshape(1, d), w_all, w_t)


def _fox_body(k_ref, qt_ref, vt_ref, gc_ref, gr_ref, brow_ref, bcol_ref, y_ref,
              cum_c, cum_r, acc_s):
    s_len = k_ref.shape[1]
    t = ATT_BLOCK
    w = GROUP_WIDTH
    nblk = s_len // t
    causal, tri_l, tri_u = _tri(t)
    key_le_query = (lax.broadcasted_iota(jnp.int32, (t, t), 0)
                    <= lax.broadcasted_iota(jnp.int32, (t, t), 1))
    row_head = lax.broadcasted_iota(jnp.int32, (w, t), 0) // HEAD_DIM
    rmask = [jnp.where(row_head == h, 1.0, 0.0).astype(BF16) for h in range(N_HEADS)]

    def cum_step(c, carry):
        car_c, car_r = carry
        r0 = pl.multiple_of(c * t, t)
        lf_c = _log_sigmoid(gc_ref[0, pl.ds(r0, t), :] + brow_ref[...])
        cc = _cumsum_rows(tri_l, lf_c) + car_c
        cum_c[pl.ds(r0, t), :] = cc
        lf_r = _log_sigmoid(gr_ref[0, :, pl.ds(r0, t)] + bcol_ref[:, 0:1])
        cr = _cumsum_lanes(lf_r, tri_u) + car_r
        cum_r[:, pl.ds(r0, t)] = cr
        return cc[t - 1:t, :], cr[:, t - 1:t]

    lax.fori_loop(0, nblk, cum_step,
                  (jnp.zeros((1, LANES), F32), jnp.zeros((GATE_ROWS, 1), F32)))

    def q_step(i, _):
        q0 = pl.multiple_of(i * t, t)
        qt = qt_ref[0, :, pl.ds(q0, t)]
        qts = [qt * rmask[h] for h in range(N_HEADS)]
        cts = [cum_r[G_FOX + h:G_FOX + h + 1, pl.ds(q0, t)] for h in range(N_HEADS)]
        acc_s[...] = jnp.zeros((w, t), F32)

        def kv_block(j, stats, diagonal):
            k0 = pl.multiple_of(j * t, t)
            kb = k_ref[0, pl.ds(k0, t), :]
            scores = [_dot(kb, qts[h]) for h in range(N_HEADS)]
            new_stats, probs, alphas = [], [], []
            for h in range(N_HEADS):
                m_old, l_old = stats[h]
                cs = cum_c[pl.ds(k0, t), G_FOX + h:G_FOX + h + 1]
                sc = scores[h] + cts[h] - cs
                if diagonal:
                    sc = jnp.where(key_le_query, sc, NEG_BIG)
                m_new = jnp.maximum(m_old, jnp.max(sc, axis=0, keepdims=True))
                alpha = jnp.exp(m_old - m_new)
                p = jnp.exp(sc - m_new)
                new_stats.append((m_new, alpha * l_old + jnp.sum(p, axis=0, keepdims=True)))
                probs.append(p.astype(BF16))
                alphas.append(alpha)
            for h in range(N_HEADS):
                hr = slice(h * HEAD_DIM, (h + 1) * HEAD_DIM)
                pv = _dot(vt_ref[0, hr, pl.ds(k0, t)], probs[h])
                acc_s[hr, :] = alphas[h] * acc_s[hr, :] + pv
            return tuple(new_stats)

        init = tuple((jnp.full((1, t), NEG_BIG, F32), jnp.zeros((1, t), F32))
                     for _ in range(N_HEADS))
        stats = lax.fori_loop(0, i, lambda j, st: kv_block(j, st, False), init)
        stats = kv_block(i, stats, True)
        out_t = jnp.concatenate(
            [acc_s[h * HEAD_DIM:(h + 1) * HEAD_DIM, :] / stats[h][1] for h in range(N_HEADS)],
            axis=0)
        y_ref[0, pl.ds(q0, t), :] = out_t.T.astype(BF16)
        return 0

    lax.fori_loop(0, nblk, q_step, 0)


def _fox(pb, pt, gc, gr, bias_row, bias_col):
    b, s, _ = pb.shape
    w = GROUP_WIDTH
    t = ATT_BLOCK
    return pl.pallas_call(
        _fox_body,
        grid=(b,),
        in_specs=[pl.BlockSpec((1, s, w), lambda bb: (bb, 0, 0)),
                  pl.BlockSpec((1, w, s), lambda bb: (bb, 0, 0)),
                  pl.BlockSpec((1, w, s), lambda bb: (bb, 1, 0)),
                  pl.BlockSpec((1, s, LANES), lambda bb: (bb, 0, 0)),
                  pl.BlockSpec((1, GATE_ROWS, s), lambda bb: (bb, 0, 0)),
                  pl.BlockSpec((1, LANES), lambda bb: (0, 0)),
                  pl.BlockSpec((GATE_ROWS, LANES), lambda bb: (0, 0))],
        out_specs=pl.BlockSpec((1, s, w), lambda bb: (bb, 0, 0)),
        out_shape=jax.ShapeDtypeStruct((b, s, w), BF16),
        scratch_shapes=[pltpu.VMEM((s, LANES), F32), pltpu.VMEM((GATE_ROWS, s), F32),
                        pltpu.VMEM((w, t), F32)],
        compiler_params=_cparams(1),
        name="fox",
    )(pb, pt, pt, gc, gr, bias_row, bias_col)


def _ret_body(p_ref, v_ref, lg_ref, inv_ref, sgn_ref, ng_ref, y_ref, cos_s, sin_s, state_s):
    s_len = p_ref.shape[1]
    n = CHUNK
    w = GROUP_WIDTH
    causal, _, _ = _tri(n)
    head_lane = _head_of_lane()
    hmask = _head_masks(head_lane)
    gmat = _group_mean_matrix()
    bd = _block_diag_mask(w)

    @pl.when(pl.program_id(0) == 0)
    def _():
        def tab(c, _):
            r0 = pl.multiple_of(c * n, n)
            pos = (lax.broadcasted_iota(jnp.int32, (n, 1), 0) + r0).astype(F32)
            ang = pos * inv_ref[...]
            cos_s[pl.ds(r0, n), :] = jnp.cos(ang)
            sin_s[pl.ds(r0, n), :] = jnp.sin(ang) * sgn_ref[...]
            return 0
        lax.fori_loop(0, s_len // n, tab, 0)

    tcol = (lax.broadcasted_iota(jnp.int32, (n, 1), 0) + 1).astype(F32)
    trow = (lax.broadcasted_iota(jnp.int32, (1, n), 1) + 1).astype(F32)
    lgs = [lg_ref[0:1, h:h + 1] for h in range(N_HEADS)]
    decays = [jnp.where(causal, jnp.exp(tcol * lgs[h] - trow * lgs[h]), 0.0) for h in range(N_HEADS)]
    q_scale = _expand([jnp.exp(tcol * lgs[h]) for h in range(N_HEADS)], head_lane)
    w_end = _expand([jnp.exp(n * lgs[h] - tcol * lgs[h]) for h in range(N_HEADS)], head_lane)
    a_chunk = _expand([jnp.exp(n * lgs[h]) for h in range(N_HEADS)], head_lane)

    state_s[...] = jnp.zeros((w, w), F32)

    def chunk(c, _):
        r0 = pl.multiple_of(c * n, n)
        rows = pl.ds(r0, n)
        cos = cos_s[rows, :]
        sin = sin_s[rows, :]
        q = p_ref[0, rows, 0:w] * cos + p_ref[0, rows, w:2 * w] * sin
        k = p_ref[0, rows, 2 * w:3 * w] * cos + p_ref[0, rows, 3 * w:4 * w] * sin
        v = v_ref[0, rows, :]
        qb = q.astype(BF16)
        kb = k.astype(BF16)
        y = _dot(qb, state_s[...].astype(BF16)) * q_scale
        for h in range(N_HEADS):
            sc = _dot_nt(qb * hmask[h], kb) * decays[h]
            y = y + _dot(sc.astype(BF16), v * hmask[h])
        kv = _dot_tn((k * w_end).astype(BF16), v)
        state_s[...] = state_s[...] * a_chunk + jnp.where(bd, kv, 0.0)
        yn = y * lax.rsqrt(_head_mean_sq(y, gmat) + RMS_EPS) * ng_ref[...]
        y_ref[0, rows, :] = (_silu(p_ref[0, rows, 4 * w:5 * w]) * yn).astype(BF16)
        return 0

    lax.fori_loop(0, s_len // n, chunk, 0)


def _ret(pr, pb, lg_row, inv_row, sgn_row, norm_g):
    b, s, _ = pr.shape
    w = GROUP_WIDTH
    return pl.pallas_call(
        _ret_body,
        grid=(b,),
        in_specs=[pl.BlockSpec((1, s, N_PR), lambda bb: (bb, 0, 0)),
                  pl.BlockSpec((1, s, w), lambda bb: (bb, 0, 2)),
                  pl.BlockSpec((1, LANES), lambda bb: (0, 0)),
                  pl.BlockSpec((1, w), lambda bb: (0, 0)),
                  pl.BlockSpec((1, w), lambda bb: (0, 0)),
                  pl.BlockSpec((1, w), lambda bb: (0, 0))],
        out_specs=pl.BlockSpec((1, s, w), lambda bb: (bb, 0, 0)),
        out_shape=jax.ShapeDtypeStruct((b, s, w), BF16),
        scratch_shapes=[pltpu.VMEM((s, w), F32), pltpu.VMEM((s, w), F32), pltpu.VMEM((w, w), F32)],
        compiler_params=_cparams(1),
        name="retention",
    )(pr, pb, lg_row, inv_row, sgn_row, norm_g.reshape(1, w))


def _ssd_body(p_ref, gc_ref, gr_ref, cw_ref, cb_ref, brow_ref, bcol_ref, arow_ref, acol_ref,
              dskip_ref, ng_ref, y_ref, state_s, tail_s):
    s_len = p_ref.shape[1]
    n = CHUNK
    w = GROUP_WIDTH
    causal, tri_l, tri_u = _tri(n)
    head_lane = _head_of_lane()
    hmask = _head_masks(head_lane)
    bd = _block_diag_mask(w)

    state_s[...] = jnp.zeros((w, w), F32)
    tail_s[...] = jnp.zeros((8, 3 * w), F32)

    def chunk(c, _):
        r0 = pl.multiple_of(c * n, n)
        rows = pl.ds(r0, n)
        pre = p_ref[0, rows, w:4 * w]
        xbc = _causal_conv_silu(pre, tail_s[...], cw_ref, cb_ref)
        tail_s[...] = pre[n - 8:n, :]
        x = xbc[:, 0:w]
        kb = xbc[:, w:2 * w].astype(BF16)
        qb = xbc[:, 2 * w:3 * w].astype(BF16)

        dt_c = _softplus(gc_ref[0, rows, :] + brow_ref[...])
        cum_c = _cumsum_rows(tri_l, dt_c * arow_ref[...])
        dt_r = _softplus(gr_ref[0, :, rows] + bcol_ref[:, 0:1])
        cum_r = _cumsum_lanes(dt_r * acol_ref[:, 0:1], tri_u)

        cts = [cum_c[:, G_DT + h:G_DT + h + 1] for h in range(N_HEADS)]
        v = x * _expand([dt_c[:, G_DT + h:G_DT + h + 1] for h in range(N_HEADS)], head_lane)
        vb = v.astype(BF16)

        y = _dot(qb, state_s[...].astype(BF16)) * _expand([jnp.exp(ct) for ct in cts], head_lane)
        heads_per_group = N_HEADS // SSD_GROUPS
        qk = [_dot_nt(qb * hmask[heads_per_group * g], kb) for g in range(SSD_GROUPS)]
        for h in range(N_HEADS):
            cs = cum_r[G_DT + h:G_DT + h + 1, :]
            sc = qk[h // heads_per_group] * jnp.where(causal, jnp.exp(cts[h] - cs), 0.0)
            y = y + _dot(sc.astype(BF16), vb * hmask[h])

        ends = [ct[n - 1:n, :] for ct in cts]
        w_end = _expand([jnp.exp(ends[h] - cts[h]) for h in range(N_HEADS)], head_lane)
        kv = _dot_tn((xbc[:, w:2 * w] * w_end).astype(BF16), vb)
        a_chunk = _expand([jnp.exp(e) for e in ends], head_lane)
        state_s[...] = state_s[...] * a_chunk + jnp.where(bd, kv, 0.0)

        hs = (y + dskip_ref[...] * x) * _silu(p_ref[0, rows, 0:w])
        yn = hs * lax.rsqrt(jnp.mean(hs * hs, axis=-1, keepdims=True) + RMS_EPS) * ng_ref[...]
        y_ref[0, rows, :] = yn.astype(BF16)
        return 0

    lax.fori_loop(0, s_len // n, chunk, 0)


def _ssd(ps, gc, gr, conv_w, conv_b, bias_row, bias_col, a_row, a_col, d_row, norm_g):
    b, s, _ = ps.shape
    w = GROUP_WIDTH

    def const(shape):
        return pl.BlockSpec(shape, lambda bb: (0,) * len(shape))

    return pl.pallas_call(
        _ssd_body,
        grid=(b,),
        in_specs=[pl.BlockSpec((1, s, N_PS), lambda bb: (bb, 0, 0)),
                  pl.BlockSpec((1, s, LANES), lambda bb: (bb, 0, 0)),
                  pl.BlockSpec((1, GATE_ROWS, s), lambda bb: (bb, 0, 0)),
                  const((CONV_K, 3 * w)), const((1, 3 * w)),
                  const((1, LANES)), const((GATE_ROWS, LANES)),
                  const((1, LANES)), const((GATE_ROWS, LANES)),
                  const((1, w)), const((1, w))],
        out_specs=pl.BlockSpec((1, s, w), lambda bb: (bb, 0, 0)),
        out_shape=jax.ShapeDtypeStruct((b, s, w), BF16),
        scratch_shapes=[pltpu.VMEM((w, w), F32), pltpu.VMEM((8, 3 * w), F32)],
        compiler_params=_cparams(1),
        name="ssd",
    )(ps, gc, gr, conv_w, conv_b, bias_row, bias_col, a_row, a_col, d_row, norm_g.reshape(1, w))


def _mlstm_body(p_ref, v_ref, gc_ref, gr_ref, cw_ref, cb_ref, brow_ref, bcol_ref, ng_ref, y_ref,
                state_s, m_s, tail_s):
    s_len = p_ref.shape[1]
    n = CHUNK
    w = GROUP_WIDTH
    wa = w + LANES
    causal, tri_l, tri_u = _tri(n)
    head_lane = _head_of_lane()
    hmask = _head_masks(head_lane)
    head_lane_aug = jnp.concatenate(
        [head_lane, lax.broadcasted_iota(jnp.int32, (1, LANES), 1)], axis=1)
    gmat = _group_mean_matrix()
    bd_aug = _block_diag_mask(wa)
    lane128 = lax.broadcasted_iota(jnp.int32, (1, LANES), 1)

    state_s[...] = jnp.zeros((w, wa), F32)
    m_s[...] = jnp.zeros((1, LANES), F32)
    tail_s[...] = jnp.zeros((8, 2 * w), F32)

    def chunk(c, _):
        r0 = pl.multiple_of(c * n, n)
        rows = pl.ds(r0, n)
        pre = p_ref[0, rows, 0:2 * w]
        qk_act = _causal_conv_silu(pre, tail_s[...], cw_ref, cb_ref)
        tail_s[...] = pre[n - 8:n, :]
        qb = (qk_act[:, 0:w] * (HEAD_DIM ** -0.5)).astype(BF16)
        k = qk_act[:, w:2 * w]
        kb = k.astype(BF16)
        vb = v_ref[0, rows, :]

        g_c = gc_ref[0, rows, :] + brow_ref[...]
        b_c = _cumsum_rows(tri_l, _log_sigmoid(g_c))
        g_r = gr_ref[0, :, rows] + bcol_ref[:, 0:1]
        b_r = _cumsum_lanes(_log_sigmoid(g_r), tri_u)

        qs = _dot(qb, state_s[...].astype(BF16))
        m_row = m_s[...]
        num = jnp.zeros((n, w), F32)
        a_inters, dens, a_olds, a_locs, w_cols = [], [], [], [], []
        m_new_row = m_row
        for h in range(N_HEADS):
            bt = b_c[:, G_MF + h:G_MF + h + 1]
            bs = b_r[G_MF + h:G_MF + h + 1, :]
            i_t = g_c[:, G_MI + h:G_MI + h + 1]
            i_s = g_r[G_MI + h:G_MI + h + 1, :]
            m_in = m_row[:, h:h + 1]
            log_d = jnp.where(causal, bt - bs + i_s, NEG_BIG)
            m_intra = jnp.max(log_d, axis=-1, keepdims=True)
            inter_log = bt + m_in
            m_t = jnp.maximum(inter_log, m_intra)
            a_inter = jnp.exp(inter_log - m_t)
            p = jnp.exp(log_d - m_t) * _dot_nt(qb * hmask[h], kb)
            num = num + _dot(p.astype(BF16), vb * hmask[h])
            den = jnp.sum(p, axis=-1, keepdims=True) + a_inter * qs[:, w + h:w + h + 1]
            a_inters.append(a_inter)
            dens.append(jnp.maximum(jnp.abs(den), jnp.exp(-m_t)))
            b_end = bt[n - 1:n, :]
            g_max = jnp.max(b_end - bs + i_s, axis=-1, keepdims=True)
            w_cols.append(jnp.exp(b_end - bt + i_t - g_max))
            m_new = jnp.maximum(b_end + m_in, g_max)
            a_olds.append(jnp.exp(b_end + m_in - m_new))
            a_locs.append(jnp.exp(g_max - m_new))
            m_new_row = jnp.where(lane128 == h, m_new, m_new_row)

        num = num + _expand(a_inters, head_lane) * qs[:, 0:w]
        hval = num / _expand(dens, head_lane)

        kw = (k * _expand(w_cols, head_lane)).astype(BF16)
        v_aug = jnp.concatenate([vb, jnp.ones((n, LANES), BF16)], axis=1)
        kv = jnp.where(bd_aug, _dot_tn(kw, v_aug), 0.0)
        state_s[...] = (state_s[...] * _expand(a_olds, head_lane_aug)
                        + kv * _expand(a_locs, head_lane_aug))
        m_s[...] = m_new_row

        yn = hval * lax.rsqrt(_head_mean_sq(hval, gmat) + RMS_EPS) * ng_ref[...]
        y_ref[0, rows, :] = (_sigmoid(p_ref[0, rows, 2 * w:3 * w]) * yn).astype(BF16)
        return 0

    lax.fori_loop(0, s_len // n, chunk, 0)


def _mlstm(pm, pb, gc, gr, conv_w, conv_b, bias_row, bias_col, norm_g):
    b, s, _ = pm.shape
    w = GROUP_WIDTH

    def const(shape):
        return pl.BlockSpec(shape, lambda bb: (0,) * len(shape))

    return pl.pallas_call(
        _mlstm_body,
        grid=(b,),
        in_specs=[pl.BlockSpec((1, s, N_PM), lambda bb: (bb, 0, 0)),
                  pl.BlockSpec((1, s, w), lambda bb: (bb, 0, 1)),
                  pl.BlockSpec((1, s, LANES), lambda bb: (bb, 0, 0)),
                  pl.BlockSpec((1, GATE_ROWS, s), lambda bb: (bb, 0, 0)),
                  const((CONV_K, 2 * w)), const((1, 2 * w)),
                  const((1, LANES)), const((GATE_ROWS, LANES)), const((1, w))],
        out_specs=pl.BlockSpec((1, s, w), lambda bb: (bb, 0, 0)),
        out_shape=jax.ShapeDtypeStruct((b, s, w), BF16),
        scratch_shapes=[pltpu.VMEM((w, w + LANES), F32), pltpu.VMEM((1, LANES), F32),
                        pltpu.VMEM((8, 2 * w), F32)],
        compiler_params=_cparams(1),
        name="mlstm",
    )(pm, pb, gc, gr, conv_w, conv_b, bias_row, bias_col, norm_g.reshape(1, w))


def _outproj_body(x_ref, gt_ref, y0_ref, y1_ref, y2_ref, y3_ref, w_ref, o_ref):
    w = GROUP_WIDTH
    acc = _dot(y0_ref[0], w_ref[0:w, :])
    acc += _dot(y1_ref[0], w_ref[w:2 * w, :])
    acc += _dot(y2_ref[0], w_ref[2 * w:3 * w, :])
    acc += _dot(y3_ref[0], w_ref[3 * w:4 * w, :])
    o_ref[0] = x_ref[0] + gt_ref[0] * acc


def _outproj(x, cond, cond_base, ys, w_out, layer):
    b, s, d = x.shape
    w = GROUP_WIDTH
    tm = FFN_ROWS
    tok = pl.BlockSpec((1, tm, d), lambda bb, i: (bb, i, 0))
    part = pl.BlockSpec((1, tm, w), lambda bb, i: (bb, i, 0))
    return pl.pallas_call(
        _outproj_body,
        grid=(b, s // tm),
        in_specs=[tok, _cond_spec(lambda bb: cond_base(bb) + 5, d), part, part, part, part,
                  pl.BlockSpec((None, N_HEADS * w, d), lambda bb, i: (layer, 0, 0))],
        out_specs=tok,
        out_shape=jax.ShapeDtypeStruct((b, s, d), F32),
        compiler_params=_cparams(2),
        name="outproj",
    )(x, cond, *ys, w_out)


def _inproj_columns():
    w, h, d = GROUP_WIDTH, N_HEADS, HEAD_DIM
    fox0 = 0
    ml0 = fox0 + 3 * w + h
    ret0 = ml0 + 4 * w + 2 * h
    ssd0 = ret0 + 4 * w
    end = ssd0 + 2 * w + 2 * SSD_GROUPS * d + h
    ar = np.arange
    swap = np.concatenate([hh * d + (ar(d) + d // 2) % d for hh in range(h)])
    rep = np.concatenate([(hh // (h // SSD_GROUPS)) * d + ar(d) for hh in range(h)])
    pad = end
    cols = np.concatenate([
        fox0 + w + ar(w),
        ml0 + 2 * w + ar(w),
        ret0 + 2 * w + ar(w),
        ml0 + ar(2 * w),
        ml0 + 3 * w + 2 * h + ar(w),
        ret0 + ar(w), ret0 + swap,
        ret0 + w + ar(w), ret0 + w + swap,
        ret0 + 3 * w + ar(w),
        ssd0 + ar(w),
        ssd0 + w + ar(w),
        ssd0 + 2 * w + rep,
        ssd0 + 2 * w + SSD_GROUPS * d + rep,
    ])
    gates = np.concatenate([fox0 + 3 * w + ar(h), ml0 + 3 * w + ar(h), ml0 + 3 * w + h + ar(h),
                            ssd0 + 2 * w + 2 * SSD_GROUPS * d + ar(h)])
    gate_cols = np.concatenate([gates, np.full(LANES - gates.size, pad)])
    scale = np.ones(cols.size + LANES, np.float32)
    k0 = N_PB + N_PM + 2 * w
    scale[k0:k0 + 2 * w] = d ** -0.5
    conv_rep = np.concatenate([ar(w), w + rep, w + SSD_GROUPS * d + rep])
    t_cols = np.concatenate([fox0 + ar(w), fox0 + 2 * w + ar(w), gates])
    t_scale = np.ones(t_cols.size, np.float32)
    t_scale[0:w] = d ** -0.5
    return np.concatenate([cols, gate_cols]), scale, t_cols, t_scale, conv_rep


_COLS, _COL_SCALE, _T_COLS, _T_SCALE, _SSD_CONV_COLS = _inproj_columns()


def _lane_pad(v, width=LANES):
    return jnp.zeros((width,), F32).at[:v.shape[0]].set(v)


def kernel(x, c, ada_w, ada_b, norm_g, ffn1_w13, ffn1_w2, ffn2_w13, ffn2_w2, w_in, w_out, fox_fb,
           mlstm_conv_w, mlstm_conv_b, mlstm_ib, mlstm_fb, mlstm_norm_g, ret_norm_g, ssd_conv_w,
           ssd_conv_b, ssd_dt_bias, ssd_A_log, ssd_D, ssd_norm_g, final_g):
    n_layers = ada_w.shape[0]
    b, s, d = x.shape
    hd = HEAD_DIM

    cond = _adaln(c, ada_w, ada_b).reshape(n_layers * b * N_MOD, 1, d)

    inv = 1.0 / (ROPE_BASE ** (jnp.arange(0, hd, 2, dtype=F32) / hd))
    inv_row = jnp.tile(inv, 2 * N_HEADS).reshape(1, GROUP_WIDTH)
    sgn_row = jnp.tile(jnp.concatenate([-jnp.ones(hd // 2, F32), jnp.ones(hd // 2, F32)]),
                       N_HEADS).reshape(1, GROUP_WIDTH)
    log_gamma = jnp.log(1.0 - 2.0 ** (-RET_DECAY_OFFSET - jnp.arange(N_HEADS, dtype=F32)))
    lg_row = _lane_pad(log_gamma).reshape(1, LANES)

    w13_a, w2_a = ffn1_w13.astype(BF16), ffn1_w2.astype(BF16)
    w13_b, w2_b = ffn2_w13.astype(BF16), ffn2_w2.astype(BF16)
    w_out_b = w_out.astype(BF16)
    w_ext = jnp.concatenate([w_in, jnp.zeros((n_layers, d, 1), F32)], axis=2)
    w_all = (w_ext[:, :, _COLS] * _COL_SCALE).astype(BF16)
    w_t = (jnp.swapaxes(w_in[:, :, _T_COLS], 1, 2) * _T_SCALE[:, None]).astype(BF16)

    for l in range(n_layers):
        def cond_base(bb, l=l):
            return (l * b + bb) * N_MOD

        x = _ffn(x, cond, cond_base, 0, norm_g[l, 0], w13_a, w2_a, l, final_g, False)
        pb, pm, pr, ps, gc, pt, gr = _inproj(x, cond, cond_base, norm_g[l, 1], w_all, w_t, l)

        gate_bias = _lane_pad(jnp.concatenate([fox_fb[l], mlstm_ib[l], mlstm_fb[l], ssd_dt_bias[l]]))
        bias_row = gate_bias.reshape(1, LANES)
        bias_col = jnp.broadcast_to(gate_bias[:GATE_ROWS, None], (GATE_ROWS, LANES))
        a_vec = _lane_pad(jnp.zeros((G_DT,), F32)).at[G_DT:G_DT + N_HEADS].set(
            -jnp.exp(ssd_A_log[l].astype(F32)))
        a_row = a_vec.reshape(1, LANES)
        a_col = jnp.broadcast_to(a_vec[:GATE_ROWS, None], (GATE_ROWS, LANES))
        d_row = jnp.repeat(ssd_D[l], hd).reshape(1, GROUP_WIDTH)

        y_fox = _fox(pb, pt, gc, gr, bias_row, bias_col)
        y_mlstm = _mlstm(pm, pb, gc, gr, mlstm_conv_w[l], mlstm_conv_b[l].reshape(1, -1),
                         bias_row, bias_col, mlstm_norm_g[l])
        y_ret = _ret(pr, pb, lg_row, inv_row, sgn_row, ret_norm_g[l])
        y_ssd = _ssd(ps, gc, gr, ssd_conv_w[l][:, _SSD_CONV_COLS],
                     ssd_conv_b[l][_SSD_CONV_COLS].reshape(1, -1), bias_row, bias_col,
                     a_row, a_col, d_row, ssd_norm_g[l])

        x = _outproj(x, cond, cond_base, (y_fox, y_mlstm, y_ret, y_ssd), w_out_b, l)
        x = _ffn(x, cond, cond_base, 6, norm_g[l, 2], w13_b, w2_b, l, final_g, l == n_layers - 1)
    return x
```

```python
import functools

import numpy as np
import jax
import jax.numpy as jnp
from jax import lax
from jax.experimental import pallas as pl
from jax.experimental.pallas import tpu as pltpu

F32 = jnp.float32
BF16 = jnp.bfloat16

N_HEADS = 4
HEAD_DIM = 64
GROUP_WIDTH = N_HEADS * HEAD_DIM
CONV_K = 4
SSD_GROUPS = 2
N_MOD = 9
ROPE_BASE = 10000.0
RET_DECAY_OFFSET = 5.0
RMS_EPS = 1e-6
NEG_BIG = -1e30

VMEM_LIMIT_BYTES = 56 * 1024 * 1024
LANES = 128
GATE_ROWS = 16

G_FOX, G_MI, G_MF, G_DT = 0, 4, 8, 12

FFN_ROWS = 512
INPROJ_ROWS = 256
ATT_BLOCK = 256
FOX_BATCH = 2
CHUNK = 128
MIX_BATCH = 2
MIX_ROWS = 512


def _cparams(n_axes):
    return pltpu.CompilerParams(dimension_semantics=("arbitrary",) * n_axes,
                                vmem_limit_bytes=VMEM_LIMIT_BYTES)


def _sigmoid(x):
    return 1.0 / (1.0 + jnp.exp(-x))


def _silu(x):
    return x * _sigmoid(x)


def _softplus(x):
    return jnp.maximum(x, 0.0) + jnp.log1p(jnp.exp(-jnp.abs(x)))


def _log_sigmoid(x):
    return -_softplus(-x)


def _rms_mod(x, g, sh, sc):
    y = x * lax.rsqrt(jnp.mean(x * x, axis=-1, keepdims=True) + RMS_EPS) * g
    return y * (1.0 + sc) + sh


def _dot(a, b):
    return jnp.dot(a, b, preferred_element_type=F32)


def _dot_nt(a, b):
    return lax.dot_general(a, b, (((1,), (1,)), ((), ())), preferred_element_type=F32)


def _dot_tn(a, b):
    return lax.dot_general(a, b, (((0,), (0,)), ((), ())), preferred_element_type=F32)


def _split3(x):
    hi = x.astype(BF16)
    r1 = x - hi.astype(F32)
    mid = r1.astype(BF16)
    lo = (r1 - mid.astype(F32)).astype(BF16)
    return hi, mid, lo


def _cumsum_rows(tri_lower, x):
    hi, mid, lo = _split3(x)
    return _dot(tri_lower, hi) + _dot(tri_lower, mid) + _dot(tri_lower, lo)


def _cumsum_lanes(x, tri_upper):
    hi, mid, lo = _split3(x)
    return _dot(hi, tri_upper) + _dot(mid, tri_upper) + _dot(lo, tri_upper)


def _tri(n):
    r = lax.broadcasted_iota(jnp.int32, (n, n), 0)
    c = lax.broadcasted_iota(jnp.int32, (n, n), 1)
    causal = r >= c
    lower = jnp.where(causal, 1.0, 0.0).astype(BF16)
    upper = jnp.where(r <= c, 1.0, 0.0).astype(BF16)
    return causal, lower, upper


def _head_of_lane(width=GROUP_WIDTH):
    return lax.broadcasted_iota(jnp.int32, (1, width), 1) // HEAD_DIM


def _head_masks(head_lane):
    return [jnp.where(head_lane == h, 1.0, 0.0).astype(BF16) for h in range(N_HEADS)]


def _expand(vals, head_lane):
    out = vals[N_HEADS - 1]
    for h in range(N_HEADS - 2, -1, -1):
        out = jnp.where(head_lane == h, vals[h], out)
    return out


def _head_mean_sq(y, gmat):
    hi, mid, lo = _split3(y * y)
    return _dot(hi, gmat) + _dot(mid, gmat) + _dot(lo, gmat)


def _group_mean_matrix():
    r = lax.broadcasted_iota(jnp.int32, (GROUP_WIDTH, GROUP_WIDTH), 0) // HEAD_DIM
    c = lax.broadcasted_iota(jnp.int32, (GROUP_WIDTH, GROUP_WIDTH), 1) // HEAD_DIM
    return jnp.where(r == c, 1.0 / HEAD_DIM, 0.0).astype(BF16)


def _block_diag_mask(cols):
    r = lax.broadcasted_iota(jnp.int32, (GROUP_WIDTH, cols), 0) // HEAD_DIM
    c = lax.broadcasted_iota(jnp.int32, (GROUP_WIDTH, cols), 1)
    same = (c < GROUP_WIDTH) & (r == c // HEAD_DIM)
    if cols > GROUP_WIDTH:
        same = same | ((c >= GROUP_WIDTH) & (r == c - GROUP_WIDTH))
    return same


def _causal_conv_silu(x, prev_tail, w_ref, b_ref):
    n = x.shape[0]
    row8 = lax.broadcasted_iota(jnp.int32, (8, 1), 0)
    x8 = x[0:8, :]
    acc = x * w_ref[CONV_K - 1:CONV_K, :] + b_ref[...]
    acc8 = acc[0:8, :]
    for j in range(1, CONV_K):
        wj = w_ref[CONV_K - 1 - j:CONV_K - j, :]
        acc = acc + pltpu.roll(x, j, 0) * wj
        acc8 = acc8 + jnp.where(row8 < j, pltpu.roll(prev_tail, j, 0), pltpu.roll(x8, j, 0)) * wj
    return _silu(jnp.concatenate([acc8, acc[8:n, :]], axis=0))


def _adaln_body(c_ref, w_ref, b_ref, o_ref):
    c = c_ref[...]
    o_ref[0] = _dot(_silu(c).astype(BF16), w_ref[0].astype(BF16)) + b_ref[0]


def _adaln(c, ada_w, ada_b):
    n_layers, d, n_out = ada_w.shape
    b = c.shape[0]
    tn = 1152
    return pl.pallas_call(
        _adaln_body,
        grid=(n_layers, n_out // tn),
        in_specs=[pl.BlockSpec((b, d), lambda l, j: (0, 0)),
                  pl.BlockSpec((1, d, tn), lambda l, j: (l, 0, j)),
                  pl.BlockSpec((1, 1, tn), lambda l, j: (l, 0, j))],
        out_specs=pl.BlockSpec((1, b, tn), lambda l, j: (l, 0, j)),
        out_shape=jax.ShapeDtypeStruct((n_layers, b, n_out), F32),
        compiler_params=_cparams(2),
        name="adaln",
    )(c, ada_w, ada_b.reshape(n_layers, 1, n_out))


def _cond_spec(cond_row, d):
    return pl.BlockSpec((1, 1, d), lambda b, i: (cond_row(b), 0, 0))


def _ffn_body(x_ref, sh_ref, sc_ref, gt_ref, ng_ref, w1_ref, w3_ref, w2_ref, fg_ref, *rest,
              apply_final, mixer_parts):
    o_ref = rest[-1]
    x = x_ref[0]
    if mixer_parts:
        mg_ref, wo_ref = rest[0], rest[1 + mixer_parts]
        w = wo_ref.shape[0] // mixer_parts
        proj = _dot(rest[1][0], wo_ref[0:w, :])
        for part in range(1, mixer_parts):
            proj += _dot(rest[1 + part][0], wo_ref[part * w:(part + 1) * w, :])
        x = x + mg_ref[0] * proj
    hb = _rms_mod(x, ng_ref[...], sh_ref[0], sc_ref[0]).astype(BF16)
    a = _dot(hb, w1_ref[...])
    g = _dot(hb, w3_ref[...])
    act = (_silu(g) * a).astype(BF16)
    y = x + (0.5 * gt_ref[0]) * _dot(act, w2_ref[...])
    if apply_final:
        y = y * lax.rsqrt(jnp.mean(y * y, axis=-1, keepdims=True) + RMS_EPS) * fg_ref[...]
    o_ref[0] = y


def _ffn(x, cond, cond_base, mod0, norm_g, w13, w2, layer, final_g, apply_final,
         mixer_ys=(), w_out=None):
    b, s, d = x.shape
    ff = w2.shape[1]
    tm = FFN_ROWS
    single = pl.Buffered(1)
    in_specs = [pl.BlockSpec((1, tm, d), lambda bb, i: (bb, i, 0)),
                _cond_spec(lambda bb: cond_base(bb) + mod0, d),
                _cond_spec(lambda bb: cond_base(bb) + mod0 + 1, d),
                _cond_spec(lambda bb: cond_base(bb) + mod0 + 2, d),
                pl.BlockSpec((1, d), lambda bb, i: (0, 0)),
                pl.BlockSpec((None, d, ff), lambda bb, i: (layer, 0, 0), pipeline_mode=single),
                pl.BlockSpec((None, d, ff), lambda bb, i: (layer, 0, 1), pipeline_mode=single),
                pl.BlockSpec((None, ff, d), lambda bb, i: (layer, 0, 0), pipeline_mode=single),
                pl.BlockSpec((1, d), lambda bb, i: (0, 0))]
    args = [x, cond, cond, cond, norm_g.reshape(1, d), w13, w13, w2, final_g.reshape(1, d)]
    if mixer_ys:
        in_specs.append(_cond_spec(lambda bb: cond_base(bb) + mod0 - 1, d))
        in_specs += [pl.BlockSpec((1, tm, y.shape[2]), lambda bb, i: (bb, i, 0)) for y in mixer_ys]
        in_specs.append(pl.BlockSpec((None,) + w_out.shape[1:], lambda bb, i: (layer, 0, 0),
                                     pipeline_mode=single))
        args += [cond, *mixer_ys, w_out]
    return pl.pallas_call(
        functools.partial(_ffn_body, apply_final=apply_final, mixer_parts=len(mixer_ys)),
        grid=(b, s // tm),
        in_specs=in_specs,
        out_specs=pl.BlockSpec((1, tm, d), lambda bb, i: (bb, i, 0)),
        out_shape=jax.ShapeDtypeStruct((b, s, d), F32),
        compiler_params=_cparams(2),
        name="ffn",
    )(*args)


N_PB = 3 * GROUP_WIDTH
N_PM = 3 * GROUP_WIDTH
N_PR = 5 * GROUP_WIDTH
N_PS = 3 * GROUP_WIDTH
N_PROJ = N_PB + N_PM + N_PR + N_PS
N_PT = 2 * GROUP_WIDTH


def _inproj_body(x_ref, sh_ref, sc_ref, ng_ref, w_ref, wt_ref,
                 pb_ref, pm_ref, pr_ref, ps_ref, pt_ref, gr_ref):
    hb = _rms_mod(x_ref[0], ng_ref[...], sh_ref[0], sc_ref[0]).astype(BF16)
    p = _dot(hb, w_ref[...])
    o = 0
    pb_ref[0] = p[:, o:o + N_PB].astype(BF16)
    o += N_PB
    pm_ref[0] = p[:, o:o + N_PM]
    o += N_PM
    pr_ref[0] = p[:, o:o + N_PR]
    o += N_PR
    ps_ref[0] = p[:, o:o + N_PS]
    pt = _dot_nt(wt_ref[...], hb)
    pt_ref[0] = pt[0:N_PT, :].astype(BF16)
    gr_ref[0] = pt[N_PT:N_PT + GATE_ROWS, :]


def _inproj(x, cond, cond_base, norm_g, w_all, w_t, layer):
    b, s, d = x.shape
    tm = INPROJ_ROWS
    single = pl.Buffered(1)

    def tok(width):
        return pl.BlockSpec((1, tm, width), lambda bb, i: (bb, i, 0))

    return pl.pallas_call(
        _inproj_body,
        grid=(b, s // tm),
        in_specs=[tok(d),
                  _cond_spec(lambda bb: cond_base(bb) + 3, d),
                  _cond_spec(lambda bb: cond_base(bb) + 4, d),
                  pl.BlockSpec((1, d), lambda bb, i: (0, 0)),
                  pl.BlockSpec((None, d, N_PROJ), lambda bb, i: (layer, 0, 0), pipeline_mode=single),
                  pl.BlockSpec((None, N_PT + GATE_ROWS, d), lambda bb, i: (layer, 0, 0))],
        out_specs=[tok(N_PB), tok(N_PM), tok(N_PR), tok(N_PS),
                   pl.BlockSpec((1, N_PT, tm), lambda bb, i: (bb, 0, i)),
                   pl.BlockSpec((1, GATE_ROWS, tm), lambda bb, i: (bb, 0, i))],
        out_shape=[jax.ShapeDtypeStruct((b, s, N_PB), BF16),
                   jax.ShapeDtypeStruct((b, s, N_PM), F32),
                   jax.ShapeDtypeStruct((b, s, N_PR), F32),
                   jax.ShapeDtypeStruct((b, s, N_PS), F32),
                   jax.ShapeDtypeStruct((b, N_PT, s), BF16),
                   jax.ShapeDtypeStruct((b, GATE_ROWS, s), F32)],
        compiler_params=_cparams(2),
        name="inproj",
    )(x, cond, cond, norm_g.reshape(1, d), w_all, w_t)


def _fox_body(k_ref, qt_ref, vt_ref, gr_ref, bcol_ref, y_ref, cum_c, cum_r, acc_s):
    nb, s_len = k_ref.shape[0], k_ref.shape[1]
    t = ATT_BLOCK
    w = GROUP_WIDTH
    nblk = s_len // t
    _, _, tri_u = _tri(t)
    key_le_query = (lax.broadcasted_iota(jnp.int32, (t, t), 0)
                    <= lax.broadcasted_iota(jnp.int32, (t, t), 1))
    row_head = lax.broadcasted_iota(jnp.int32, (w, t), 0) // HEAD_DIM
    rmask = [jnp.where(row_head == h, 1.0, 0.0).astype(BF16) for h in range(N_HEADS)]
    units = [(bi, h) for bi in range(nb) for h in range(N_HEADS)]

    def cum_step(c, carry):
        r0 = pl.multiple_of(c * t, t)
        out = []
        for bi in range(nb):
            lf_r = _log_sigmoid(gr_ref[bi, :, pl.ds(r0, t)] + bcol_ref[:, 0:1])
            cr = _cumsum_lanes(lf_r, tri_u) + carry[bi]
            cum_r[bi, :, pl.ds(r0, t)] = cr
            cum_c[bi, pl.ds(r0, t), :] = cr.T
            out.append(cr[:, t - 1:t])
        return tuple(out)

    lax.fori_loop(0, nblk, cum_step, tuple(jnp.zeros((GATE_ROWS, 1), F32) for _ in range(nb)))

    def q_step(i, _):
        q0 = pl.multiple_of(i * t, t)
        qts, cts = [], []
        for bi, h in units:
            qts.append(qt_ref[bi, :, pl.ds(q0, t)] * rmask[h])
            cts.append(cum_r[bi, G_FOX + h:G_FOX + h + 1, pl.ds(q0, t)])
        acc_s[...] = jnp.zeros((nb, w, t), F32)

        def kv_block(j, stats, diagonal):
            k0 = pl.multiple_of(j * t, t)
            kbs = [k_ref[bi, pl.ds(k0, t), :] for bi in range(nb)]
            scores = [_dot(kbs[bi], qts[u]) for u, (bi, h) in enumerate(units)]
            new_stats, probs, alphas = [], [], []
            for u, (bi, h) in enumerate(units):
                m_old, l_old = stats[u]
                cs = cum_c[bi, pl.ds(k0, t), G_FOX + h:G_FOX + h + 1]
                sc = scores[u] + cts[u] - cs
                if diagonal:
                    sc = jnp.where(key_le_query, sc, NEG_BIG)
                m_new = jnp.maximum(m_old, jnp.max(sc, axis=0, keepdims=True))
                alpha = jnp.exp(m_old - m_new)
                p = jnp.exp(sc - m_new)
                new_stats.append((m_new, alpha * l_old + jnp.sum(p, axis=0, keepdims=True)))
                probs.append(p.astype(BF16))
                alphas.append(alpha)
            for u, (bi, h) in enumerate(units):
                hr = slice(h * HEAD_DIM, (h + 1) * HEAD_DIM)
                pv = _dot(vt_ref[bi, hr, pl.ds(k0, t)], probs[u])
                acc_s[bi, hr, :] = alphas[u] * acc_s[bi, hr, :] + pv
            return tuple(new_stats)

        init = tuple((jnp.full((1, t), NEG_BIG, F32), jnp.zeros((1, t), F32)) for _ in units)
        stats = lax.fori_loop(0, i, lambda j, st: kv_block(j, st, False), init)
        stats = kv_block(i, stats, True)
        for bi in range(nb):
            out_t = jnp.concatenate(
                [acc_s[bi, h * HEAD_DIM:(h + 1) * HEAD_DIM, :] / stats[bi * N_HEADS + h][1]
                 for h in range(N_HEADS)], axis=0)
            y_ref[bi, pl.ds(q0, t), :] = out_t.T.astype(BF16)
        return 0

    lax.fori_loop(0, nblk, q_step, 0)


def _fox(pb, pt, gr, bias_col):
    b, s, _ = pb.shape
    w = GROUP_WIDTH
    t = ATT_BLOCK
    nb = FOX_BATCH
    return pl.pallas_call(
        _fox_body,
        grid=(b // nb,),
        in_specs=[pl.BlockSpec((nb, s, w), lambda bb: (bb, 0, 0)),
                  pl.BlockSpec((nb, w, s), lambda bb: (bb, 0, 0)),
                  pl.BlockSpec((nb, w, s), lambda bb: (bb, 1, 0)),
                  pl.BlockSpec((nb, GATE_ROWS, s), lambda bb: (bb, 0, 0)),
                  pl.BlockSpec((GATE_ROWS, LANES), lambda bb: (0, 0))],
        out_specs=pl.BlockSpec((nb, s, w), lambda bb: (bb, 0, 0)),
        out_shape=jax.ShapeDtypeStruct((b, s, w), BF16),
        scratch_shapes=[pltpu.VMEM((nb, s, GATE_ROWS), F32), pltpu.VMEM((nb, GATE_ROWS, s), F32),
                        pltpu.VMEM((nb, w, t), F32)],
        compiler_params=_cparams(1),
        name="fox",
    )(pb, pt, pt, gr, bias_col)


def _lockstep(gens):
    gens = list(gens)
    while gens:
        alive = []
        for g in gens:
            try:
                next(g)
                alive.append(g)
            except StopIteration:
                pass
        gens = alive


def _ret_body(p_ref, v_ref, lg_ref, inv_ref, sgn_ref, ng_ref, y_ref, cos_s, sin_s, state_s):
    nb, ts = p_ref.shape[0], p_ref.shape[1]
    n = CHUNK
    w = GROUP_WIDTH
    si = pl.program_id(1)
    base = pl.multiple_of(si * ts, ts)
    causal, _, _ = _tri(n)
    head_lane = _head_of_lane()
    hmask = _head_masks(head_lane)
    gmat = _group_mean_matrix()
    bd = _block_diag_mask(w)

    @pl.when(pl.program_id(0) == 0)
    def _():
        def tab(c, _):
            r0 = pl.multiple_of(base + c * n, n)
            pos = (lax.broadcasted_iota(jnp.int32, (n, 1), 0) + r0).astype(F32)
            ang = pos * inv_ref[...]
            cos_s[pl.ds(r0, n), :] = jnp.cos(ang)
            sin_s[pl.ds(r0, n), :] = jnp.sin(ang) * sgn_ref[...]
            return 0
        lax.fori_loop(0, ts // n, tab, 0)

    @pl.when(si == 0)
    def _():
        state_s[...] = jnp.zeros((nb, w, w), F32)

    tcol = (lax.broadcasted_iota(jnp.int32, (n, 1), 0) + 1).astype(F32)
    trow = (lax.broadcasted_iota(jnp.int32, (1, n), 1) + 1).astype(F32)
    lgs = [lg_ref[0:1, h:h + 1] for h in range(N_HEADS)]
    decays = [jnp.where(causal, jnp.exp(tcol * lgs[h] - trow * lgs[h]), 0.0) for h in range(N_HEADS)]
    q_scale = _expand([jnp.exp(tcol * lgs[h]) for h in range(N_HEADS)], head_lane)
    w_end = _expand([jnp.exp(n * lgs[h] - tcol * lgs[h]) for h in range(N_HEADS)], head_lane)
    a_chunk = _expand([jnp.exp(n * lgs[h]) for h in range(N_HEADS)], head_lane)

    def per_batch(bi, c):
        r0 = pl.multiple_of(c * n, n)
        rows = pl.ds(r0, n)
        trows = pl.ds(pl.multiple_of(base + r0, n), n)
        cos = cos_s[trows, :]
        sin = sin_s[trows, :]
        q = p_ref[bi, rows, 0:w] * cos + p_ref[bi, rows, w:2 * w] * sin
        k = p_ref[bi, rows, 2 * w:3 * w] * cos + p_ref[bi, rows, 3 * w:4 * w] * sin
        v = v_ref[bi, rows, :]
        qb = q.astype(BF16)
        kb = k.astype(BF16)
        qs = _dot(qb, state_s[bi].astype(BF16))
        qk = _dot_nt(jnp.concatenate([qb * hmask[h] for h in range(N_HEADS)], axis=0), kb)
        yield
        probs = [(qk[h * n:(h + 1) * n, :] * decays[h]).astype(BF16) for h in range(N_HEADS)]
        pv = _dot(jnp.concatenate(probs, axis=1),
                  jnp.concatenate([v * hmask[h] for h in range(N_HEADS)], axis=0))
        kv = _dot_tn((k * w_end).astype(BF16), v)
        yield
        y = qs * q_scale + pv
        state_s[bi] = state_s[bi] * a_chunk + jnp.where(bd, kv, 0.0)
        msq = _head_mean_sq(y, gmat)
        yield
        yn = y * lax.rsqrt(msq + RMS_EPS) * ng_ref[...]
        y_ref[bi, rows, :] = (_silu(p_ref[bi, rows, 4 * w:5 * w]) * yn).astype(BF16)

    def chunk(c, _):
        _lockstep(per_batch(bi, c) for bi in range(nb))
        return 0

    lax.fori_loop(0, ts // n, chunk, 0)


def _mixer_specs(widths_and_blocks):
    return [pl.BlockSpec((MIX_BATCH, MIX_ROWS, width), lambda bb, ss, blk=blk: (bb, ss, blk))
            for width, blk in widths_and_blocks]


def _mixer_const(shape):
    return pl.BlockSpec(shape, lambda bb, ss: (0,) * len(shape))


def _ret(pr, pb, lg_row, inv_row, sgn_row, norm_g):
    b, s, _ = pr.shape
    w = GROUP_WIDTH
    return pl.pallas_call(
        _ret_body,
        grid=(b // MIX_BATCH, s // MIX_ROWS),
        in_specs=_mixer_specs([(N_PR, 0), (w, 2)]) + [
            _mixer_const((1, LANES)), _mixer_const((1, w)), _mixer_const((1, w)),
            _mixer_const((1, w))],
        out_specs=_mixer_specs([(w, 0)])[0],
        out_shape=jax.ShapeDtypeStruct((b, s, w), BF16),
        scratch_shapes=[pltpu.VMEM((s, w), F32), pltpu.VMEM((s, w), F32),
                        pltpu.VMEM((MIX_BATCH, w, w), F32)],
        compiler_params=_cparams(2),
        name="retention",
    )(pr, pb, lg_row, inv_row, sgn_row, norm_g.reshape(1, w))


def _ssd_body(p_ref, gr_ref, cw_ref, cb_ref, bcol_ref, acol_ref, dskip_ref, ng_ref, y_ref,
              state_s, tail_s):
    nb, ts = p_ref.shape[0], p_ref.shape[1]
    n = CHUNK
    w = GROUP_WIDTH
    causal, _, tri_u = _tri(n)
    head_lane = _head_of_lane()
    hmask = _head_masks(head_lane)
    gw = SSD_GROUPS * HEAD_DIM
    hpg = N_HEADS // SSD_GROUPS
    group_lane = lax.broadcasted_iota(jnp.int32, (1, gw), 1) // HEAD_DIM
    gmask = [jnp.where(group_lane == g, 1.0, 0.0).astype(BF16) for g in range(SSD_GROUPS)]
    state_mask = (lax.broadcasted_iota(jnp.int32, (gw, w), 0) // HEAD_DIM
                  == lax.broadcasted_iota(jnp.int32, (gw, w), 1) // (HEAD_DIM * hpg))

    @pl.when(pl.program_id(1) == 0)
    def _():
        state_s[...] = jnp.zeros((nb, gw, w), F32)
        tail_s[...] = jnp.zeros((nb, 8, 2 * w), F32)

    def per_batch(bi, c):
        r0 = pl.multiple_of(c * n, n)
        rows = pl.ds(r0, n)
        pre = p_ref[bi, rows, w:3 * w]
        xbc = _causal_conv_silu(pre, tail_s[bi], cw_ref, cb_ref)
        tail_s[bi] = pre[n - 8:n, :]
        x = xbc[:, 0:w]
        kb = xbc[:, w:w + gw].astype(BF16)
        qb = xbc[:, w + gw:w + 2 * gw].astype(BF16)

        qs = _dot(qb, state_s[bi].astype(BF16))
        qk = _dot_nt(jnp.concatenate([qb * gmask[g] for g in range(SSD_GROUPS)], axis=0), kb)

        dt_r = _softplus(gr_ref[bi, :, rows] + bcol_ref[:, 0:1])
        yield
        cum_r = _cumsum_lanes(dt_r * acol_ref[:, 0:1], tri_u)
        yield
        cols = jnp.concatenate([dt_r, cum_r], axis=0).T
        yield

        cts = [cols[:, GATE_ROWS + G_DT + h:GATE_ROWS + G_DT + h + 1] for h in range(N_HEADS)]
        ends = [ct[n - 1:n, :] for ct in cts]
        v = x * _expand([cols[:, G_DT + h:G_DT + h + 1] for h in range(N_HEADS)], head_lane)
        vb = v.astype(BF16)

        css = [cum_r[G_DT + h:G_DT + h + 1, :] for h in range(N_HEADS)]
        decays = [jnp.where(causal, jnp.exp(cts[h] - css[h]), 0.0) for h in range(N_HEADS)]
        probs = [(qk[(h // hpg) * n:(h // hpg + 1) * n, :] * decays[h]).astype(BF16)
                 for h in range(N_HEADS)]
        pv = _dot(jnp.concatenate(probs, axis=1),
                  jnp.concatenate([vb * hmask[h] for h in range(N_HEADS)], axis=0))
        w_end = _expand([jnp.exp(ends[h] - cts[h]) for h in range(N_HEADS)], head_lane)
        kv = _dot_tn(kb, (v * w_end).astype(BF16))
        yield
        y = qs * _expand([jnp.exp(ct) for ct in cts], head_lane) + pv
        a_chunk = _expand([jnp.exp(e) for e in ends], head_lane)
        state_s[bi] = state_s[bi] * a_chunk + jnp.where(state_mask, kv, 0.0)

        hs = (y + dskip_ref[...] * x) * _silu(p_ref[bi, rows, 0:w])
        yn = hs * lax.rsqrt(jnp.mean(hs * hs, axis=-1, keepdims=True) + RMS_EPS) * ng_ref[...]
        y_ref[bi, rows, :] = yn.astype(BF16)

    def chunk(c, _):
        _lockstep(per_batch(bi, c) for bi in range(nb))
        return 0

    lax.fori_loop(0, ts // n, chunk, 0)


def _gate_row_spec():
    return pl.BlockSpec((MIX_BATCH, GATE_ROWS, MIX_ROWS), lambda bb, ss: (bb, 0, ss))


def _ssd(ps, gr, conv_w, conv_b, bias_col, a_col, d_row, norm_g):
    b, s, _ = ps.shape
    w = GROUP_WIDTH
    const = _mixer_const
    return pl.pallas_call(
        _ssd_body,
        grid=(b // MIX_BATCH, s // MIX_ROWS),
        in_specs=_mixer_specs([(N_PS, 0)]) + [
            _gate_row_spec(),
            const((CONV_K, 2 * w)), const((1, 2 * w)),
            const((GATE_ROWS, LANES)), const((GATE_ROWS, LANES)),
            const((1, w)), const((1, w))],
        out_specs=_mixer_specs([(w, 0)])[0],
        out_shape=jax.ShapeDtypeStruct((b, s, w), BF16),
        scratch_shapes=[pltpu.VMEM((MIX_BATCH, SSD_GROUPS * HEAD_DIM, w), F32),
                        pltpu.VMEM((MIX_BATCH, 8, 2 * w), F32)],
        compiler_params=_cparams(2),
        name="ssd",
    )(ps, gr, conv_w, conv_b, bias_col, a_col, d_row, norm_g.reshape(1, w))


def _chunk_cumsum_matrix(ts, n):
    r = lax.broadcasted_iota(jnp.int32, (ts, ts), 0)
    c = lax.broadcasted_iota(jnp.int32, (ts, ts), 1)
    return jnp.where((r <= c) & (r // n == c // n), 1.0, 0.0).astype(BF16)


def _mlstm_body(p_ref, v_ref, gr_ref, cw_ref, cb_ref, bcol_ref, ng_ref, y_ref,
                state_s, m_s, tail_s, q_s, k_s, rows_s, cols_s, h_s):
    nb, ts = p_ref.shape[0], p_ref.shape[1]
    n = CHUNK
    w = GROUP_WIDTH
    wa = w + LANES
    causal, _, _ = _tri(n)
    head_lane = _head_of_lane()
    hmask = _head_masks(head_lane)
    head_lane_aug = jnp.concatenate(
        [head_lane, lax.broadcasted_iota(jnp.int32, (1, LANES), 1)], axis=1)
    gmat = _group_mean_matrix()
    bd_aug = _block_diag_mask(wa)
    lane128 = lax.broadcasted_iota(jnp.int32, (1, LANES), 1)

    @pl.when(pl.program_id(1) == 0)
    def _():
        state_s[...] = jnp.zeros((nb, w, wa), F32)
        m_s[...] = jnp.zeros((nb, 1, LANES), F32)
        tail_s[...] = jnp.zeros((nb, 8, 2 * w), F32)

    tri_blk = _chunk_cumsum_matrix(ts, n)
    for bi in range(nb):
        pre = p_ref[bi, :, 0:2 * w]
        qk_act = _causal_conv_silu(pre, tail_s[bi], cw_ref, cb_ref)
        tail_s[bi] = pre[ts - 8:ts, :]
        q_s[bi] = (qk_act[:, 0:w] * (HEAD_DIM ** -0.5)).astype(BF16)
        k_s[bi] = qk_act[:, w:2 * w]
        g_blk = gr_ref[bi] + bcol_ref[:, 0:1]
        gate_rows = jnp.concatenate([g_blk, _cumsum_lanes(_log_sigmoid(g_blk), tri_blk)], axis=0)
        rows_s[bi] = gate_rows
        cols_s[bi] = gate_rows.T

    def per_batch(bi, c):
        r0 = pl.multiple_of(c * n, n)
        rows = pl.ds(r0, n)
        qb = q_s[bi, rows, :]
        k = k_s[bi, rows, :]
        kb = k.astype(BF16)
        vb = v_ref[bi, rows, :]

        qs = _dot(qb, state_s[bi].astype(BF16))
        qk = _dot_nt(jnp.concatenate([qb * hmask[h] for h in range(N_HEADS)], axis=0), kb)
        yield

        g_r = rows_s[bi, 0:GATE_ROWS, rows]
        b_r = rows_s[bi, GATE_ROWS:2 * GATE_ROWS, rows]
        cols = cols_s[bi, rows, :]
        g_c = cols[:, 0:GATE_ROWS]
        b_c = cols[:, GATE_ROWS:2 * GATE_ROWS]
        m_row = m_s[bi]
        heads = range(N_HEADS)
        bts = [b_c[:, G_MF + h:G_MF + h + 1] for h in heads]
        bss = [b_r[G_MF + h:G_MF + h + 1, :] for h in heads]
        i_ts = [g_c[:, G_MI + h:G_MI + h + 1] for h in heads]
        i_ss = [g_r[G_MI + h:G_MI + h + 1, :] for h in heads]
        m_ins = [m_row[:, h:h + 1] for h in heads]
        b_ends = [bts[h][n - 1:n, :] for h in heads]
        log_ds = [jnp.where(causal, bts[h] - bss[h] + i_ss[h], NEG_BIG) for h in heads]
        m_intras = [jnp.max(log_ds[h], axis=-1, keepdims=True) for h in heads]
        g_maxs = [jnp.max(b_ends[h] - bss[h] + i_ss[h], axis=-1, keepdims=True) for h in heads]
        yield
        inter_logs = [bts[h] + m_ins[h] for h in heads]
        m_ts = [jnp.maximum(inter_logs[h], m_intras[h]) for h in heads]
        a_inters = [jnp.exp(inter_logs[h] - m_ts[h]) for h in heads]
        ps = [jnp.exp(log_ds[h] - m_ts[h]) * qk[h * n:(h + 1) * n, :] for h in heads]
        probs = [p.astype(BF16) for p in ps]
        psums = [jnp.sum(p, axis=-1, keepdims=True) for p in ps]
        w_cols = [jnp.exp(b_ends[h] - bts[h] + i_ts[h] - g_maxs[h]) for h in heads]
        m_news = [jnp.maximum(b_ends[h] + m_ins[h], g_maxs[h]) for h in heads]
        a_olds = [jnp.exp(b_ends[h] + m_ins[h] - m_news[h]) for h in heads]
        a_locs = [jnp.exp(g_maxs[h] - m_news[h]) for h in heads]
        m_new_row = m_row
        for h in heads:
            m_new_row = jnp.where(lane128 == h, m_news[h], m_new_row)

        pv = _dot(jnp.concatenate(probs, axis=1),
                  jnp.concatenate([vb * hmask[h] for h in range(N_HEADS)], axis=0))
        kw = (k * _expand(w_cols, head_lane)).astype(BF16)
        v_aug = jnp.concatenate([vb, jnp.ones((n, LANES), BF16)], axis=1)
        kv = _dot_tn(kw, v_aug)
        yield
        num = pv + _expand(a_inters, head_lane) * qs[:, 0:w]
        dens = []
        for h in range(N_HEADS):
            den = psums[h] + a_inters[h] * qs[:, w + h:w + h + 1]
            dens.append(jnp.maximum(jnp.abs(den), jnp.exp(-m_ts[h])))
        h_s[bi, rows, :] = num / _expand(dens, head_lane)
        state_s[bi] = (state_s[bi] * _expand(a_olds, head_lane_aug)
                       + jnp.where(bd_aug, kv, 0.0) * _expand(a_locs, head_lane_aug))
        m_s[bi] = m_new_row

    def chunk(c, _):
        _lockstep(per_batch(bi, c) for bi in range(nb))
        return 0

    lax.fori_loop(0, ts // n, chunk, 0)

    for bi in range(nb):
        hval = h_s[bi]
        yn = hval * lax.rsqrt(_head_mean_sq(hval, gmat) + RMS_EPS) * ng_ref[...]
        y_ref[bi] = (_sigmoid(p_ref[bi, :, 2 * w:3 * w]) * yn).astype(BF16)


def _mlstm(pm, pb, gr, conv_w, conv_b, bias_col, norm_g):
    b, s, _ = pm.shape
    w = GROUP_WIDTH
    const = _mixer_const
    return pl.pallas_call(
        _mlstm_body,
        grid=(b // MIX_BATCH, s // MIX_ROWS),
        in_specs=_mixer_specs([(N_PM, 0), (w, 1)]) + [
            _gate_row_spec(),
            const((CONV_K, 2 * w)), const((1, 2 * w)),
            const((GATE_ROWS, LANES)), const((1, w))],
        out_specs=_mixer_specs([(w, 0)])[0],
        out_shape=jax.ShapeDtypeStruct((b, s, w), BF16),
        scratch_shapes=[pltpu.VMEM((MIX_BATCH, w, w + LANES), F32),
                        pltpu.VMEM((MIX_BATCH, 1, LANES), F32),
                        pltpu.VMEM((MIX_BATCH, 8, 2 * w), F32),
                        pltpu.VMEM((MIX_BATCH, MIX_ROWS, w), BF16),
                        pltpu.VMEM((MIX_BATCH, MIX_ROWS, w), F32),
                        pltpu.VMEM((MIX_BATCH, 2 * GATE_ROWS, MIX_ROWS), F32),
                        pltpu.VMEM((MIX_BATCH, MIX_ROWS, 2 * GATE_ROWS), F32),
                        pltpu.VMEM((MIX_BATCH, MIX_ROWS, w), F32)],
        compiler_params=_cparams(2),
        name="mlstm",
    )(pm, pb, gr, conv_w, conv_b, bias_col, norm_g.reshape(1, w))


def _half_swap(t):
    lead = t.shape[:-1]
    t = t.reshape(lead + (N_HEADS, 2, HEAD_DIM // 2))
    return t[..., ::-1, :].reshape(lead + (GROUP_WIDTH,))


def _inproj_weights(w_in):
    w, h, hd = GROUP_WIDTH, N_HEADS, HEAD_DIM
    n_layers, d, _ = w_in.shape
    qk_scale = hd ** -0.5
    fox0 = 0
    ml0 = fox0 + 3 * w + h
    ret0 = ml0 + 4 * w + 2 * h
    ssd0 = ret0 + 4 * w
    bc = 2 * SSD_GROUPS * hd

    def col(start, width):
        return w_in[:, :, start:start + width]

    rq, rk = col(ret0, w), col(ret0 + w, w) * qk_scale
    gates = jnp.concatenate([col(fox0 + 3 * w, h), col(ml0 + 3 * w, 2 * h), col(ssd0 + 2 * w + bc, h)],
                            axis=2)
    w_all = jnp.concatenate([
        col(fox0 + w, w), col(ml0 + 2 * w, w), col(ret0 + 2 * w, w),
        col(ml0, 2 * w), col(ml0 + 3 * w + 2 * h, w),
        rq, _half_swap(rq), rk, _half_swap(rk), col(ret0 + 3 * w, w),
        col(ssd0, w), col(ssd0 + w, w + bc)], axis=2)
    w_t = jnp.concatenate([col(fox0, w) * qk_scale, col(fox0 + 2 * w, w), gates], axis=2)
    return w_all.astype(BF16), jnp.swapaxes(w_t, 1, 2).astype(BF16)


def _lane_pad(v, width=LANES):
    return jnp.zeros((width,), F32).at[:v.shape[0]].set(v)


def kernel(x, c, ada_w, ada_b, norm_g, ffn1_w13, ffn1_w2, ffn2_w13, ffn2_w2, w_in, w_out, fox_fb,
           mlstm_conv_w, mlstm_conv_b, mlstm_ib, mlstm_fb, mlstm_norm_g, ret_norm_g, ssd_conv_w,
           ssd_conv_b, ssd_dt_bias, ssd_A_log, ssd_D, ssd_norm_g, final_g):
    n_layers = ada_w.shape[0]
    b, s, d = x.shape
    hd = HEAD_DIM

    cond = _adaln(c, ada_w, ada_b).reshape(n_layers * b * N_MOD, 1, d)

    inv = 1.0 / (ROPE_BASE ** (jnp.arange(0, hd, 2, dtype=F32) / hd))
    inv_row = jnp.tile(inv, 2 * N_HEADS).reshape(1, GROUP_WIDTH)
    sgn_row = jnp.tile(jnp.concatenate([-jnp.ones(hd // 2, F32), jnp.ones(hd // 2, F32)]),
                       N_HEADS).reshape(1, GROUP_WIDTH)
    log_gamma = jnp.log(1.0 - 2.0 ** (-RET_DECAY_OFFSET - jnp.arange(N_HEADS, dtype=F32)))
    lg_row = _lane_pad(log_gamma).reshape(1, LANES)

    w13_a, w2_a = ffn1_w13.astype(BF16), ffn1_w2.astype(BF16)
    w13_b, w2_b = ffn2_w13.astype(BF16), ffn2_w2.astype(BF16)
    w_out_b = w_out.astype(BF16)
    w_all, w_t = _inproj_weights(w_in)

    for l in range(n_layers):
        def cond_base(bb, l=l):
            return (l * b + bb) * N_MOD

        x = _ffn(x, cond, cond_base, 0, norm_g[l, 0], w13_a, w2_a, l, final_g, False)
        pb, pm, pr, ps, pt, gr = _inproj(x, cond, cond_base, norm_g[l, 1], w_all, w_t, l)

        gate_bias = jnp.concatenate([fox_fb[l], mlstm_ib[l], mlstm_fb[l], ssd_dt_bias[l]])
        bias_col = jnp.broadcast_to(gate_bias[:, None], (GATE_ROWS, LANES))
        a_vec = jnp.zeros((GATE_ROWS,), F32).at[G_DT:G_DT + N_HEADS].set(
            -jnp.exp(ssd_A_log[l].astype(F32)))
        a_col = jnp.broadcast_to(a_vec[:, None], (GATE_ROWS, LANES))
        d_row = jnp.repeat(ssd_D[l], hd).reshape(1, GROUP_WIDTH)

        y_fox = _fox(pb, pt, gr, bias_col)
        y_mlstm = _mlstm(pm, pb, gr, mlstm_conv_w[l], mlstm_conv_b[l].reshape(1, -1),
                         bias_col, mlstm_norm_g[l])
        y_ret = _ret(pr, pb, lg_row, inv_row, sgn_row, ret_norm_g[l])
        y_ssd = _ssd(ps, gr, ssd_conv_w[l], ssd_conv_b[l].reshape(1, -1), bias_col,
                     a_col, d_row, ssd_norm_g[l])

        x = _ffn(x, cond, cond_base, 6, norm_g[l, 2], w13_b, w2_b, l, final_g, l == n_layers - 1,
                 mixer_ys=(y_fox, y_mlstm, y_ret, y_ssd), w_out=w_out_b)
    return x
```

```python
import functools

import numpy as np
import jax
import jax.numpy as jnp
from jax import lax
from jax.experimental import pallas as pl
from jax.experimental.pallas import tpu as pltpu

F32 = jnp.float32
BF16 = jnp.bfloat16

N_HEADS = 4
HEAD_DIM = 64
GROUP_WIDTH = N_HEADS * HEAD_DIM
CONV_K = 4
SSD_GROUPS = 2
N_MOD = 9
ROPE_BASE = 10000.0
RET_DECAY_OFFSET = 5.0
RMS_EPS = 1e-6
NEG_BIG = -1e30

VMEM_LIMIT_BYTES = 56 * 1024 * 1024
LANES = 128
GATE_ROWS = 16

G_FOX, G_MI, G_MF, G_DT = 0, 4, 8, 12

FFN_ROWS = 512
FFN_SPLIT = 2
INPROJ_ROWS = 256
ATT_BLOCK = 256
FOX_BATCH = 4
CHUNK = 128
MIX_BATCH = 4
MIX_ROWS = 512


def _cparams(n_axes):
    return pltpu.CompilerParams(dimension_semantics=("arbitrary",) * n_axes,
                                vmem_limit_bytes=VMEM_LIMIT_BYTES)


def _sigmoid(x):
    return 1.0 / (1.0 + jnp.exp(-x))


def _silu(x):
    return x * _sigmoid(x)


def _softplus(x):
    return jnp.maximum(x, 0.0) + jnp.log1p(jnp.exp(-jnp.abs(x)))


def _log_sigmoid(x):
    return -_softplus(-x)


def _rms_mod(x, g, sh, sc):
    y = x * lax.rsqrt(jnp.mean(x * x, axis=-1, keepdims=True) + RMS_EPS) * g
    return y * (1.0 + sc) + sh


def _dot(a, b):
    return jnp.dot(a, b, preferred_element_type=F32)


def _dot_nt(a, b):
    return lax.dot_general(a, b, (((1,), (1,)), ((), ())), preferred_element_type=F32)


def _dot_tn(a, b):
    return lax.dot_general(a, b, (((0,), (0,)), ((), ())), preferred_element_type=F32)


def _split3(x):
    hi = x.astype(BF16)
    r1 = x - hi.astype(F32)
    mid = r1.astype(BF16)
    lo = (r1 - mid.astype(F32)).astype(BF16)
    return hi, mid, lo


def _cumsum_rows(tri_lower, x):
    hi, mid, lo = _split3(x)
    return _dot(tri_lower, hi) + _dot(tri_lower, mid) + _dot(tri_lower, lo)


def _cumsum_lanes(x, tri_upper):
    return _dot(jnp.concatenate(_split3(x), axis=1),
                jnp.concatenate([tri_upper, tri_upper, tri_upper], axis=0))


def _tri(n):
    r = lax.broadcasted_iota(jnp.int32, (n, n), 0)
    c = lax.broadcasted_iota(jnp.int32, (n, n), 1)
    causal = r >= c
    lower = jnp.where(causal, 1.0, 0.0).astype(BF16)
    upper = jnp.where(r <= c, 1.0, 0.0).astype(BF16)
    return causal, lower, upper


def _head_of_lane(width=GROUP_WIDTH):
    return lax.broadcasted_iota(jnp.int32, (1, width), 1) // HEAD_DIM


def _head_masks(head_lane):
    return [jnp.where(head_lane == h, 1.0, 0.0).astype(BF16) for h in range(N_HEADS)]


def _expand(vals, head_lane):
    out = vals[N_HEADS - 1]
    for h in range(N_HEADS - 2, -1, -1):
        out = jnp.where(head_lane == h, vals[h], out)
    return out


def _head_mean_sq(y, gmat):
    return _dot(jnp.concatenate(_split3(y * y), axis=1), jnp.concatenate([gmat, gmat, gmat], axis=0))


def _group_mean_matrix():
    r = lax.broadcasted_iota(jnp.int32, (GROUP_WIDTH, GROUP_WIDTH), 0) // HEAD_DIM
    c = lax.broadcasted_iota(jnp.int32, (GROUP_WIDTH, GROUP_WIDTH), 1) // HEAD_DIM
    return jnp.where(r == c, 1.0 / HEAD_DIM, 0.0).astype(BF16)


def _block_diag_mask(cols):
    r = lax.broadcasted_iota(jnp.int32, (GROUP_WIDTH, cols), 0) // HEAD_DIM
    c = lax.broadcasted_iota(jnp.int32, (GROUP_WIDTH, cols), 1)
    same = (c < GROUP_WIDTH) & (r == c // HEAD_DIM)
    if cols > GROUP_WIDTH:
        same = same | ((c >= GROUP_WIDTH) & (r == c - GROUP_WIDTH))
    return same


def _causal_conv_silu(x, prev_tail, w_ref, b_ref):
    n = x.shape[0]
    row8 = lax.broadcasted_iota(jnp.int32, (8, 1), 0)
    x8 = x[0:8, :]
    acc = x * w_ref[CONV_K - 1:CONV_K, :] + b_ref[...]
    acc8 = acc[0:8, :]
    for j in range(1, CONV_K):
        wj = w_ref[CONV_K - 1 - j:CONV_K - j, :]
        acc = acc + pltpu.roll(x, j, 0) * wj
        acc8 = acc8 + jnp.where(row8 < j, pltpu.roll(prev_tail, j, 0), pltpu.roll(x8, j, 0)) * wj
    return _silu(jnp.concatenate([acc8, acc[8:n, :]], axis=0))


def _adaln_body(c_ref, w_ref, b_ref, o_ref):
    c = c_ref[...]
    o_ref[0] = _dot(_silu(c).astype(BF16), w_ref[0].astype(BF16)) + b_ref[0]


def _adaln(c, ada_w, ada_b):
    n_layers, d, n_out = ada_w.shape
    b = c.shape[0]
    tn = 1152
    return pl.pallas_call(
        _adaln_body,
        grid=(n_layers, n_out // tn),
        in_specs=[pl.BlockSpec((b, d), lambda l, j: (0, 0)),
                  pl.BlockSpec((1, d, tn), lambda l, j: (l, 0, j)),
                  pl.BlockSpec((1, 1, tn), lambda l, j: (l, 0, j))],
        out_specs=pl.BlockSpec((1, b, tn), lambda l, j: (l, 0, j)),
        out_shape=jax.ShapeDtypeStruct((n_layers, b, n_out), F32),
        compiler_params=_cparams(2),
        name="adaln",
    )(c, ada_w, ada_b.reshape(n_layers, 1, n_out))


def _cond_spec(cond_row, d):
    return pl.BlockSpec((1, 1, d), lambda b, i: (cond_row(b), 0, 0))


def _ffn_body(x_ref, sh_ref, sc_ref, gt_ref, ng_ref, w1_ref, w3_ref, w2_ref, fg_ref, *rest,
              apply_final, mixer_parts):
    o_ref = rest[-1]
    tm = x_ref.shape[1]
    sub = tm // FFN_SPLIT

    def row_group(r):
        rs = slice(r * sub, (r + 1) * sub)
        x = x_ref[0, rs, :]
        if mixer_parts:
            mg_ref, wo_ref = rest[0], rest[1 + mixer_parts]
            w = wo_ref.shape[0] // mixer_parts
            proj = _dot(rest[1][0, rs, :], wo_ref[0:w, :])
            for part in range(1, mixer_parts):
                proj += _dot(rest[1 + part][0, rs, :], wo_ref[part * w:(part + 1) * w, :])
            yield
            x = x + mg_ref[0] * proj
        hb = _rms_mod(x, ng_ref[...], sh_ref[0], sc_ref[0]).astype(BF16)
        yield
        a = _dot(hb, w1_ref[...])
        g = _dot(hb, w3_ref[...])
        yield
        act = (_silu(g) * a).astype(BF16)
        yield
        down = _dot(act, w2_ref[...])
        yield
        y = x + (0.5 * gt_ref[0]) * down
        if apply_final:
            y = y * lax.rsqrt(jnp.mean(y * y, axis=-1, keepdims=True) + RMS_EPS) * fg_ref[...]
        o_ref[0, rs, :] = y

    _lockstep(row_group(r) for r in range(FFN_SPLIT))


def _ffn(x, cond, cond_base, mod0, norm_g, w13, w2, layer, final_g, apply_final,
         mixer_ys=(), w_out=None):
    b, s, d = x.shape
    ff = w2.shape[1]
    tm = FFN_ROWS
    single = pl.Buffered(1)
    in_specs = [pl.BlockSpec((1, tm, d), lambda bb, i: (bb, i, 0)),
                _cond_spec(lambda bb: cond_base(bb) + mod0, d),
                _cond_spec(lambda bb: cond_base(bb) + mod0 + 1, d),
                _cond_spec(lambda bb: cond_base(bb) + mod0 + 2, d),
                pl.BlockSpec((1, d), lambda bb, i: (0, 0)),
                pl.BlockSpec((None, d, ff), lambda bb, i: (layer, 0, 0), pipeline_mode=single),
                pl.BlockSpec((None, d, ff), lambda bb, i: (layer, 0, 1), pipeline_mode=single),
                pl.BlockSpec((None, ff, d), lambda bb, i: (layer, 0, 0), pipeline_mode=single),
                pl.BlockSpec((1, d), lambda bb, i: (0, 0))]
    args = [x, cond, cond, cond, norm_g.reshape(1, d), w13, w13, w2, final_g.reshape(1, d)]
    if mixer_ys:
        in_specs.append(_cond_spec(lambda bb: cond_base(bb) + mod0 - 1, d))
        in_specs += [pl.BlockSpec((1, tm, y.shape[2]), lambda bb, i: (bb, i, 0)) for y in mixer_ys]
        in_specs.append(pl.BlockSpec((None,) + w_out.shape[1:], lambda bb, i: (layer, 0, 0),
                                     pipeline_mode=single))
        args += [cond, *mixer_ys, w_out]
    return pl.pallas_call(
        functools.partial(_ffn_body, apply_final=apply_final, mixer_parts=len(mixer_ys)),
        grid=(b, s // tm),
        in_specs=in_specs,
        out_specs=pl.BlockSpec((1, tm, d), lambda bb, i: (bb, i, 0)),
        out_shape=jax.ShapeDtypeStruct((b, s, d), F32),
        compiler_params=_cparams(2),
        name="ffn",
    )(*args)


N_PB = 3 * GROUP_WIDTH
N_PM = 3 * GROUP_WIDTH
N_PR = 5 * GROUP_WIDTH
N_PS = 3 * GROUP_WIDTH
N_PROJ = N_PB + N_PM + N_PR + N_PS
N_PT = 2 * GROUP_WIDTH


def _inproj_body(x_ref, sh_ref, sc_ref, ng_ref, w_ref, wt_ref,
                 pb_ref, pm_ref, pr_ref, ps_ref, pt_ref, gr_ref):
    hb = _rms_mod(x_ref[0], ng_ref[...], sh_ref[0], sc_ref[0]).astype(BF16)
    p = _dot(hb, w_ref[...])
    o = 0
    pb_ref[0] = p[:, o:o + N_PB].astype(BF16)
    o += N_PB
    pm_ref[0] = p[:, o:o + N_PM]
    o += N_PM
    pr_ref[0] = p[:, o:o + N_PR]
    o += N_PR
    ps_ref[0] = p[:, o:o + N_PS]
    pt = _dot_nt(wt_ref[...], hb)
    pt_ref[0] = pt[0:N_PT, :].astype(BF16)
    gr_ref[0] = pt[N_PT:N_PT + GATE_ROWS, :]


def _inproj(x, cond, cond_base, norm_g, w_all, w_t, layer):
    b, s, d = x.shape
    tm = INPROJ_ROWS
    single = pl.Buffered(1)

    def tok(width):
        return pl.BlockSpec((1, tm, width), lambda bb, i: (bb, i, 0))

    return pl.pallas_call(
        _inproj_body,
        grid=(b, s // tm),
        in_specs=[tok(d),
                  _cond_spec(lambda bb: cond_base(bb) + 3, d),
                  _cond_spec(lambda bb: cond_base(bb) + 4, d),
                  pl.BlockSpec((1, d), lambda bb, i: (0, 0)),
                  pl.BlockSpec((None, d, N_PROJ), lambda bb, i: (layer, 0, 0), pipeline_mode=single),
                  pl.BlockSpec((None, N_PT + GATE_ROWS, d), lambda bb, i: (layer, 0, 0))],
        out_specs=[tok(N_PB), tok(N_PM), tok(N_PR), tok(N_PS),
                   pl.BlockSpec((1, N_PT, tm), lambda bb, i: (bb, 0, i)),
                   pl.BlockSpec((1, GATE_ROWS, tm), lambda bb, i: (bb, 0, i))],
        out_shape=[jax.ShapeDtypeStruct((b, s, N_PB), BF16),
                   jax.ShapeDtypeStruct((b, s, N_PM), F32),
                   jax.ShapeDtypeStruct((b, s, N_PR), F32),
                   jax.ShapeDtypeStruct((b, s, N_PS), F32),
                   jax.ShapeDtypeStruct((b, N_PT, s), BF16),
                   jax.ShapeDtypeStruct((b, GATE_ROWS, s), F32)],
        compiler_params=_cparams(2),
        name="inproj",
    )(x, cond, cond, norm_g.reshape(1, d), w_all, w_t)


def _fox_body(k_ref, qt_ref, vt_ref, gr_ref, bcol_ref, y_ref, cum_c, cum_r, acc_s):
    nb, s_len = k_ref.shape[0], k_ref.shape[1]
    t = ATT_BLOCK
    w = GROUP_WIDTH
    nblk = s_len // t
    _, _, tri_u = _tri(t)
    key_le_query = (lax.broadcasted_iota(jnp.int32, (t, t), 0)
                    <= lax.broadcasted_iota(jnp.int32, (t, t), 1))
    row_head = lax.broadcasted_iota(jnp.int32, (w, t), 0) // HEAD_DIM
    rmask = [jnp.where(row_head == h, 1.0, 0.0).astype(BF16) for h in range(N_HEADS)]
    units = [(bi, h) for bi in range(nb) for h in range(N_HEADS)]

    def cum_step(c, carry):
        r0 = pl.multiple_of(c * t, t)
        out = []
        for bi in range(nb):
            lf_r = _log_sigmoid(gr_ref[bi, :, pl.ds(r0, t)] + bcol_ref[:, 0:1])
            cr = _cumsum_lanes(lf_r, tri_u) + carry[bi]
            cum_r[bi, :, pl.ds(r0, t)] = cr
            cum_c[bi, pl.ds(r0, t), :] = cr.T
            out.append(cr[:, t - 1:t])
        return tuple(out)

    lax.fori_loop(0, nblk, cum_step, tuple(jnp.zeros((GATE_ROWS, 1), F32) for _ in range(nb)))

    def q_step(i, _):
        q0 = pl.multiple_of(i * t, t)
        qts, cts = [], []
        for bi, h in units:
            qts.append(qt_ref[bi, :, pl.ds(q0, t)] * rmask[h])
            cts.append(cum_r[bi, G_FOX + h:G_FOX + h + 1, pl.ds(q0, t)])
        acc_s[...] = jnp.zeros((nb, w, t), F32)

        def kv_block(j, stats, diagonal):
            k0 = pl.multiple_of(j * t, t)
            kbs = [k_ref[bi, pl.ds(k0, t), :] for bi in range(nb)]
            scores = [_dot(kbs[bi], qts[u]) for u, (bi, h) in enumerate(units)]
            new_stats, probs, alphas = [], [], []
            for u, (bi, h) in enumerate(units):
                m_old, l_old = stats[u]
                cs = cum_c[bi, pl.ds(k0, t), G_FOX + h:G_FOX + h + 1]
                sc = scores[u] + cts[u] - cs
                if diagonal:
                    sc = jnp.where(key_le_query, sc, NEG_BIG)
                m_new = jnp.maximum(m_old, jnp.max(sc, axis=0, keepdims=True))
                alpha = jnp.exp(m_old - m_new)
                p = jnp.exp(sc - m_new)
                new_stats.append((m_new, alpha * l_old + jnp.sum(p, axis=0, keepdims=True)))
                probs.append(p.astype(BF16))
                alphas.append(alpha)
            for u, (bi, h) in enumerate(units):
                hr = slice(h * HEAD_DIM, (h + 1) * HEAD_DIM)
                pv = _dot(vt_ref[bi, hr, pl.ds(k0, t)], probs[u])
                acc_s[bi, hr, :] = alphas[u] * acc_s[bi, hr, :] + pv
            return tuple(new_stats)

        init = tuple((jnp.full((1, t), NEG_BIG, F32), jnp.zeros((1, t), F32)) for _ in units)
        stats = lax.fori_loop(0, i, lambda j, st: kv_block(j, st, False), init)
        stats = kv_block(i, stats, True)
        for bi in range(nb):
            out_t = jnp.concatenate(
                [acc_s[bi, h * HEAD_DIM:(h + 1) * HEAD_DIM, :] / stats[bi * N_HEADS + h][1]
                 for h in range(N_HEADS)], axis=0)
            y_ref[bi, pl.ds(q0, t), :] = out_t.T.astype(BF16)
        return 0

    lax.fori_loop(0, nblk, q_step, 0)


def _fox(pb, pt, gr, bias_col):
    b, s, _ = pb.shape
    w = GROUP_WIDTH
    t = ATT_BLOCK
    nb = FOX_BATCH
    return pl.pallas_call(
        _fox_body,
        grid=(b // nb,),
        in_specs=[pl.BlockSpec((nb, s, w), lambda bb: (bb, 0, 0)),
                  pl.BlockSpec((nb, w, s), lambda bb: (bb, 0, 0)),
                  pl.BlockSpec((nb, w, s), lambda bb: (bb, 1, 0)),
                  pl.BlockSpec((nb, GATE_ROWS, s), lambda bb: (bb, 0, 0)),
                  pl.BlockSpec((GATE_ROWS, LANES), lambda bb: (0, 0))],
        out_specs=pl.BlockSpec((nb, s, w), lambda bb: (bb, 0, 0)),
        out_shape=jax.ShapeDtypeStruct((b, s, w), BF16),
        scratch_shapes=[pltpu.VMEM((nb, s, GATE_ROWS), F32), pltpu.VMEM((nb, GATE_ROWS, s), F32),
                        pltpu.VMEM((nb, w, t), F32)],
        compiler_params=_cparams(1),
        name="fox",
    )(pb, pt, pt, gr, bias_col)


def _lockstep(gens):
    gens = list(gens)
    while gens:
        alive = []
        for g in gens:
            try:
                next(g)
                alive.append(g)
            except StopIteration:
                pass
        gens = alive


def _ret_body(p_ref, v_ref, lg_ref, inv_ref, sgn_ref, ng_ref, y_ref, cos_s, sin_s, state_s):
    nb, ts = p_ref.shape[0], p_ref.shape[1]
    n = CHUNK
    w = GROUP_WIDTH
    si = pl.program_id(1)
    base = pl.multiple_of(si * ts, ts)
    causal, _, _ = _tri(n)
    head_lane = _head_of_lane()
    hmask = _head_masks(head_lane)
    gmat = _group_mean_matrix()
    bd = _block_diag_mask(w)

    @pl.when(pl.program_id(0) == 0)
    def _():
        def tab(c, _):
            r0 = pl.multiple_of(base + c * n, n)
            pos = (lax.broadcasted_iota(jnp.int32, (n, 1), 0) + r0).astype(F32)
            ang = pos * inv_ref[...]
            cos_s[pl.ds(r0, n), :] = jnp.cos(ang)
            sin_s[pl.ds(r0, n), :] = jnp.sin(ang) * sgn_ref[...]
            return 0
        lax.fori_loop(0, ts // n, tab, 0)

    @pl.when(si == 0)
    def _():
        state_s[...] = jnp.zeros((nb, w, w), F32)

    tcol = (lax.broadcasted_iota(jnp.int32, (n, 1), 0) + 1).astype(F32)
    trow = (lax.broadcasted_iota(jnp.int32, (1, n), 1) + 1).astype(F32)
    lgs = [lg_ref[0:1, h:h + 1] for h in range(N_HEADS)]
    decays = [jnp.where(causal, jnp.exp(tcol * lgs[h] - trow * lgs[h]), 0.0) for h in range(N_HEADS)]
    q_scale = _expand([jnp.exp(tcol * lgs[h]) for h in range(N_HEADS)], head_lane)
    w_end = _expand([jnp.exp(n * lgs[h] - tcol * lgs[h]) for h in range(N_HEADS)], head_lane)
    a_chunk = _expand([jnp.exp(n * lgs[h]) for h in range(N_HEADS)], head_lane)

    def per_batch(bi, c):
        r0 = pl.multiple_of(c * n, n)
        rows = pl.ds(r0, n)
        trows = pl.ds(pl.multiple_of(base + r0, n), n)
        cos = cos_s[trows, :]
        sin = sin_s[trows, :]
        q = p_ref[bi, rows, 0:w] * cos + p_ref[bi, rows, w:2 * w] * sin
        k = p_ref[bi, rows, 2 * w:3 * w] * cos + p_ref[bi, rows, 3 * w:4 * w] * sin
        v = v_ref[bi, rows, :]
        qb = q.astype(BF16)
        kb = k.astype(BF16)
        qs = _dot(qb, state_s[bi].astype(BF16))
        qk = _dot_nt(jnp.concatenate([qb * hmask[h] for h in range(N_HEADS)], axis=0), kb)
        yield
        probs = [(qk[h * n:(h + 1) * n, :] * decays[h]).astype(BF16) for h in range(N_HEADS)]
        pv = _dot(jnp.concatenate(probs, axis=1),
                  jnp.concatenate([v * hmask[h] for h in range(N_HEADS)], axis=0))
        kv = _dot_tn((k * w_end).astype(BF16), v)
        yield
        y = qs * q_scale + pv
        state_s[bi] = state_s[bi] * a_chunk + jnp.where(bd, kv, 0.0)
        msq = _head_mean_sq(y, gmat)
        yield
        yn = y * lax.rsqrt(msq + RMS_EPS) * ng_ref[...]
        y_ref[bi, rows, :] = (_silu(p_ref[bi, rows, 4 * w:5 * w]) * yn).astype(BF16)

    def chunk(c, _):
        _lockstep(per_batch(bi, c) for bi in range(nb))
        return 0

    lax.fori_loop(0, ts // n, chunk, 0)


def _mixer_specs(widths_and_blocks):
    return [pl.BlockSpec((MIX_BATCH, MIX_ROWS, width), lambda bb, ss, blk=blk: (bb, ss, blk))
            for width, blk in widths_and_blocks]


def _mixer_const(shape):
    return pl.BlockSpec(shape, lambda bb, ss: (0,) * len(shape))


def _ret(pr, pb, lg_row, inv_row, sgn_row, norm_g):
    b, s, _ = pr.shape
    w = GROUP_WIDTH
    return pl.pallas_call(
        _ret_body,
        grid=(b // MIX_BATCH, s // MIX_ROWS),
        in_specs=_mixer_specs([(N_PR, 0), (w, 2)]) + [
            _mixer_const((1, LANES)), _mixer_const((1, w)), _mixer_const((1, w)),
            _mixer_const((1, w))],
        out_specs=_mixer_specs([(w, 0)])[0],
        out_shape=jax.ShapeDtypeStruct((b, s, w), BF16),
        scratch_shapes=[pltpu.VMEM((s, w), F32), pltpu.VMEM((s, w), F32),
                        pltpu.VMEM((MIX_BATCH, w, w), F32)],
        compiler_params=_cparams(2),
        name="retention",
    )(pr, pb, lg_row, inv_row, sgn_row, norm_g.reshape(1, w))


def _ssd_body(p_ref, gr_ref, cw_ref, cb_ref, bcol_ref, acol_ref, dskip_ref, ng_ref, y_ref,
              state_s, tail_s):
    nb, ts = p_ref.shape[0], p_ref.shape[1]
    n = CHUNK
    w = GROUP_WIDTH
    causal, _, tri_u = _tri(n)
    head_lane = _head_of_lane()
    hmask = _head_masks(head_lane)
    gw = SSD_GROUPS * HEAD_DIM
    hpg = N_HEADS // SSD_GROUPS
    group_lane = lax.broadcasted_iota(jnp.int32, (1, gw), 1) // HEAD_DIM
    gmask = [jnp.where(group_lane == g, 1.0, 0.0).astype(BF16) for g in range(SSD_GROUPS)]
    state_mask = (lax.broadcasted_iota(jnp.int32, (gw, w), 0) // HEAD_DIM
                  == lax.broadcasted_iota(jnp.int32, (gw, w), 1) // (HEAD_DIM * hpg))

    @pl.when(pl.program_id(1) == 0)
    def _():
        state_s[...] = jnp.zeros((nb, gw, w), F32)
        tail_s[...] = jnp.zeros((nb, 8, 2 * w), F32)

    def per_batch(bi, c):
        r0 = pl.multiple_of(c * n, n)
        rows = pl.ds(r0, n)
        pre = p_ref[bi, rows, w:3 * w]
        xbc = _causal_conv_silu(pre, tail_s[bi], cw_ref, cb_ref)
        tail_s[bi] = pre[n - 8:n, :]
        x = xbc[:, 0:w]
        kb = xbc[:, w:w + gw].astype(BF16)
        qb = xbc[:, w + gw:w + 2 * gw].astype(BF16)

        qs = _dot(qb, state_s[bi].astype(BF16))
        qk = _dot_nt(jnp.concatenate([qb * gmask[g] for g in range(SSD_GROUPS)], axis=0), kb)

        dt_r = _softplus(gr_ref[bi, :, rows] + bcol_ref[:, 0:1])
        yield
        cum_r = _cumsum_lanes(dt_r * acol_ref[:, 0:1], tri_u)
        yield
        cols = jnp.concatenate([dt_r, cum_r], axis=0).T
        yield

        cts = [cols[:, GATE_ROWS + G_DT + h:GATE_ROWS + G_DT + h + 1] for h in range(N_HEADS)]
        ends = [ct[n - 1:n, :] for ct in cts]
        v = x * _expand([cols[:, G_DT + h:G_DT + h + 1] for h in range(N_HEADS)], head_lane)
        vb = v.astype(BF16)

        css = [cum_r[G_DT + h:G_DT + h + 1, :] for h in range(N_HEADS)]
        decays = [jnp.where(causal, jnp.exp(cts[h] - css[h]), 0.0) for h in range(N_HEADS)]
        probs = [(qk[(h // hpg) * n:(h // hpg + 1) * n, :] * decays[h]).astype(BF16)
                 for h in range(N_HEADS)]
        pv = _dot(jnp.concatenate(probs, axis=1),
                  jnp.concatenate([vb * hmask[h] for h in range(N_HEADS)], axis=0))
        w_end = _expand([jnp.exp(ends[h] - cts[h]) for h in range(N_HEADS)], head_lane)
        kv = _dot_tn(kb, (v * w_end).astype(BF16))
        yield
        y = qs * _expand([jnp.exp(ct) for ct in cts], head_lane) + pv
        a_chunk = _expand([jnp.exp(e) for e in ends], head_lane)
        state_s[bi] = state_s[bi] * a_chunk + jnp.where(state_mask, kv, 0.0)

        hs = (y + dskip_ref[...] * x) * _silu(p_ref[bi, rows, 0:w])
        yn = hs * lax.rsqrt(jnp.mean(hs * hs, axis=-1, keepdims=True) + RMS_EPS) * ng_ref[...]
        y_ref[bi, rows, :] = yn.astype(BF16)

    def chunk(c, _):
        _lockstep(per_batch(bi, c) for bi in range(nb))
        return 0

    lax.fori_loop(0, ts // n, chunk, 0)


def _gate_row_spec():
    return pl.BlockSpec((MIX_BATCH, GATE_ROWS, MIX_ROWS), lambda bb, ss: (bb, 0, ss))


def _ssd(ps, gr, conv_w, conv_b, bias_col, a_col, d_row, norm_g):
    b, s, _ = ps.shape
    w = GROUP_WIDTH
    const = _mixer_const
    return pl.pallas_call(
        _ssd_body,
        grid=(b // MIX_BATCH, s // MIX_ROWS),
        in_specs=_mixer_specs([(N_PS, 0)]) + [
            _gate_row_spec(),
            const((CONV_K, 2 * w)), const((1, 2 * w)),
            const((GATE_ROWS, LANES)), const((GATE_ROWS, LANES)),
            const((1, w)), const((1, w))],
        out_specs=_mixer_specs([(w, 0)])[0],
        out_shape=jax.ShapeDtypeStruct((b, s, w), BF16),
        scratch_shapes=[pltpu.VMEM((MIX_BATCH, SSD_GROUPS * HEAD_DIM, w), F32),
                        pltpu.VMEM((MIX_BATCH, 8, 2 * w), F32)],
        compiler_params=_cparams(2),
        name="ssd",
    )(ps, gr, conv_w, conv_b, bias_col, a_col, d_row, norm_g.reshape(1, w))


def _rows_to_lanes(x8, sel):
    parts = _split3(jnp.concatenate([x8, jnp.zeros_like(x8)], axis=0))
    return _dot_tn(jnp.concatenate(parts, axis=0), jnp.concatenate([sel, sel, sel], axis=0))


def _chunk_cummax(xs, n):
    lane_in_chunk = lax.broadcasted_iota(jnp.int32, (1, xs[0].shape[1]), 1) % n
    shift = 1
    while shift < n:
        keep = lane_in_chunk >= shift
        xs = [jnp.maximum(x, jnp.where(keep, pltpu.roll(x, shift, 1), NEG_BIG)) for x in xs]
        shift *= 2
    return xs


def _mlstm_body(p_ref, v_ref, gr_ref, cw_ref, cb_ref, bcol_ref, ng_ref, y_ref,
                state_s, m_s, tail_s, q_s, k_s, h_s, rrow_s, colb_s, colc_s, cmat_s, cexp_s,
                rexp_s):
    nb, ts = p_ref.shape[0], p_ref.shape[1]
    n = CHUNK
    w = GROUP_WIDTH
    wa = w + LANES
    causal, _, _ = _tri(n)
    head_lane = _head_of_lane()
    hmask = _head_masks(head_lane)
    head_lane_aug = jnp.concatenate(
        [head_lane, lax.broadcasted_iota(jnp.int32, (1, LANES), 1)], axis=1)
    gmat = _group_mean_matrix()
    bd_aug = _block_diag_mask(wa)
    lane128 = lax.broadcasted_iota(jnp.int32, (1, LANES), 1)
    ones_col = [jnp.where(lane128 == h, 1.0, 0.0).astype(BF16) for h in range(N_HEADS)]

    @pl.when(pl.program_id(1) == 0)
    def _():
        state_s[...] = jnp.zeros((nb, w, wa), F32)
        m_s[...] = jnp.zeros((nb, 1, 8), F32)
        tail_s[...] = jnp.zeros((nb, 8, 2 * w), F32)

    _, _, tri_u = _tri(n)
    sel_row = lax.broadcasted_iota(jnp.int32, (16, 1), 0)
    sel_chunk = jnp.where(
        sel_row == lax.broadcasted_iota(jnp.int32, (1, N_HEADS * n), 1) // n, 1.0, 0.0).astype(BF16)
    sel_head = jnp.where(sel_row == head_lane, 1.0, 0.0).astype(BF16)
    batch = range(nb)
    n_chunks = ts // n
    g_blks = [gr_ref[bi] + bcol_ref[:, 0:1] for bi in batch]
    lf_blks = [_log_sigmoid(g) for g in g_blks]
    b_stack = _cumsum_lanes(
        jnp.concatenate([lf[:, c * n:(c + 1) * n] for lf in lf_blks for c in range(n_chunks)], axis=0),
        tri_u)
    b16s = [jnp.concatenate([b_stack[(bi * n_chunks + c) * GATE_ROWS:(bi * n_chunks + c + 1) * GATE_ROWS]
                             for c in range(n_chunks)], axis=1) for bi in batch]
    r16s = [pltpu.roll(g_blks[bi], G_MF - G_MI, 0) - b16s[bi] for bi in batch]
    b8s = [b16[G_MF:G_MF + 8, :] for b16 in b16s]
    r8s = [r16[G_MF:G_MF + 8, :] for r16 in r16s]
    c8s = _chunk_cummax(r8s, n)
    c_mats = [_rows_to_lanes(c8, sel_chunk) for c8 in c8s]
    c_exps = [_rows_to_lanes(c8, sel_head) for c8 in c8s]
    r_exps = [_rows_to_lanes(r8, sel_head) for r8 in r8s]
    for bi in batch:
        rrow_s[bi] = r8s[bi]
        colb_s[bi] = b8s[bi].T
        colc_s[bi] = c8s[bi].T
        for h in range(N_HEADS):
            cmat_s[bi, h] = c_mats[bi][:, h * n:(h + 1) * n]
        cexp_s[bi] = c_exps[bi]
        rexp_s[bi] = r_exps[bi]
    def conv_step(c, _):
        rows = pl.ds(pl.multiple_of(c * n, n), n)
        for bi in batch:
            pre = p_ref[bi, rows, 0:2 * w]
            qk_act = _causal_conv_silu(pre, tail_s[bi], cw_ref, cb_ref)
            tail_s[bi] = pre[n - 8:n, :]
            q_s[bi, rows, :] = (qk_act[:, 0:w] * (HEAD_DIM ** -0.5)).astype(BF16)
            k_s[bi, rows, :] = qk_act[:, w:2 * w]
        return 0

    lax.fori_loop(0, n_chunks, conv_step, 0)

    def per_batch(bi, c):
        r0 = pl.multiple_of(c * n, n)
        rows = pl.ds(r0, n)
        qb = q_s[bi, rows, :]
        k = k_s[bi, rows, :]
        kb = k.astype(BF16)
        vb = v_ref[bi, rows, :]

        qs = _dot(qb, state_s[bi].astype(BF16))
        qk = _dot_nt(jnp.concatenate([qb * hmask[h] for h in range(N_HEADS)], axis=0), kb)
        yield

        heads = range(N_HEADS)
        m8 = m_s[bi]
        m_ins = [m8[:, h:h + 1] for h in heads]
        r_rows = rrow_s[bi, :, rows]
        col_b = colb_s[bi, rows, :]
        col_c = colc_s[bi, rows, :]
        c_end8 = col_c[n - 1:n, :]
        z_end8 = jnp.maximum(m8, c_end8)
        probs = [(jnp.where(causal,
                            jnp.exp(r_rows[h:h + 1, :] - jnp.maximum(m_ins[h], cmat_s[bi, h, rows, :])),
                            0.0) * qk[h * n:(h + 1) * n, :]).astype(BF16) for h in heads]
        v_rows = jnp.concatenate(
            [jnp.concatenate([vb * hmask[h], jnp.broadcast_to(ones_col[h], (n, LANES))], axis=1)
             for h in heads], axis=0)
        pv = _dot(jnp.concatenate(probs, axis=1), v_rows)
        c_ends = [c_end8[:, h:h + 1] for h in heads]
        kw = (k * jnp.exp(rexp_s[bi, rows, :] - _expand(c_ends, head_lane))).astype(BF16)
        v_aug = jnp.concatenate([vb, jnp.ones((n, LANES), BF16)], axis=1)
        kv = _dot_tn(kw, v_aug)
        yield
        m_lanes = _expand(m_ins, head_lane)
        a_inter = jnp.exp(m_lanes - jnp.maximum(m_lanes, cexp_s[bi, rows, :]))
        num = pv[:, 0:w] + a_inter * qs[:, 0:w]
        z8 = jnp.maximum(m8, col_c)
        den8 = pv[:, w:w + 8] + jnp.exp(m8 - z8) * qs[:, w:w + 8]
        dens8 = jnp.maximum(jnp.abs(den8), jnp.exp(-(col_b + z8)))
        h_s[bi, rows, :] = num / _expand([dens8[:, h:h + 1] for h in heads], head_lane)
        a_old8 = jnp.exp(m8 - z_end8)
        a_loc8 = jnp.exp(c_end8 - z_end8)
        state_s[bi] = (state_s[bi] * _expand([a_old8[:, h:h + 1] for h in heads], head_lane_aug)
                       + jnp.where(bd_aug, kv, 0.0)
                       * _expand([a_loc8[:, h:h + 1] for h in heads], head_lane_aug))
        m_s[bi] = col_b[n - 1:n, :] + z_end8

    def chunk(c, _):
        _lockstep(per_batch(bi, c) for bi in range(nb))
        return 0

    lax.fori_loop(0, ts // n, chunk, 0)

    def norm_step(c, _):
        rows = pl.ds(pl.multiple_of(c * n, n), n)
        hvals = [h_s[bi, rows, :] for bi in batch]
        msqs = [_head_mean_sq(hv, gmat) for hv in hvals]
        for bi in batch:
            yn = hvals[bi] * lax.rsqrt(msqs[bi] + RMS_EPS) * ng_ref[...]
            y_ref[bi, rows, :] = (_sigmoid(p_ref[bi, rows, 2 * w:3 * w]) * yn).astype(BF16)
        return 0

    lax.fori_loop(0, n_chunks, norm_step, 0)


def _mlstm(pm, pb, gr, conv_w, conv_b, bias_col, norm_g):
    b, s, _ = pm.shape
    w = GROUP_WIDTH
    const = _mixer_const
    return pl.pallas_call(
        _mlstm_body,
        grid=(b // MIX_BATCH, s // MIX_ROWS),
        in_specs=_mixer_specs([(N_PM, 0), (w, 1)]) + [
            _gate_row_spec(),
            const((CONV_K, 2 * w)), const((1, 2 * w)),
            const((GATE_ROWS, LANES)), const((1, w))],
        out_specs=_mixer_specs([(w, 0)])[0],
        out_shape=jax.ShapeDtypeStruct((b, s, w), BF16),
        scratch_shapes=[pltpu.VMEM((MIX_BATCH, w, w + LANES), F32),
                        pltpu.VMEM((MIX_BATCH, 1, 8), F32),
                        pltpu.VMEM((MIX_BATCH, 8, 2 * w), F32),
                        pltpu.VMEM((MIX_BATCH, MIX_ROWS, w), BF16),
                        pltpu.VMEM((MIX_BATCH, MIX_ROWS, w), F32),
                        pltpu.VMEM((MIX_BATCH, MIX_ROWS, w), F32),
                        pltpu.VMEM((MIX_BATCH, 8, MIX_ROWS), F32),
                        pltpu.VMEM((MIX_BATCH, MIX_ROWS, 8), F32),
                        pltpu.VMEM((MIX_BATCH, MIX_ROWS, 8), F32),
                        pltpu.VMEM((MIX_BATCH, N_HEADS, MIX_ROWS, CHUNK), F32),
                        pltpu.VMEM((MIX_BATCH, MIX_ROWS, w), F32),
                        pltpu.VMEM((MIX_BATCH, MIX_ROWS, w), F32)],
        compiler_params=_cparams(2),
        name="mlstm",
    )(pm, pb, gr, conv_w, conv_b, bias_col, norm_g.reshape(1, w))


def _half_swap(t):
    lead = t.shape[:-1]
    t = t.reshape(lead + (N_HEADS, 2, HEAD_DIM // 2))
    return t[..., ::-1, :].reshape(lead + (GROUP_WIDTH,))


def _inproj_weights(w_in):
    w, h, hd = GROUP_WIDTH, N_HEADS, HEAD_DIM
    n_layers, d, _ = w_in.shape
    qk_scale = hd ** -0.5
    fox0 = 0
    ml0 = fox0 + 3 * w + h
    ret0 = ml0 + 4 * w + 2 * h
    ssd0 = ret0 + 4 * w
    bc = 2 * SSD_GROUPS * hd

    def col(start, width):
        return w_in[:, :, start:start + width]

    rq, rk = col(ret0, w), col(ret0 + w, w) * qk_scale
    gates = jnp.concatenate([col(fox0 + 3 * w, h), col(ml0 + 3 * w, 2 * h), col(ssd0 + 2 * w + bc, h)],
                            axis=2)
    w_all = jnp.concatenate([
        col(fox0 + w, w), col(ml0 + 2 * w, w), col(ret0 + 2 * w, w),
        col(ml0, 2 * w), col(ml0 + 3 * w + 2 * h, w),
        rq, _half_swap(rq), rk, _half_swap(rk), col(ret0 + 3 * w, w),
        col(ssd0, w), col(ssd0 + w, w + bc)], axis=2)
    w_t = jnp.concatenate([col(fox0, w) * qk_scale, col(fox0 + 2 * w, w), gates], axis=2)
    return w_all.astype(BF16), jnp.swapaxes(w_t, 1, 2).astype(BF16)


def _lane_pad(v, width=LANES):
    return jnp.zeros((width,), F32).at[:v.shape[0]].set(v)


def kernel(x, c, ada_w, ada_b, norm_g, ffn1_w13, ffn1_w2, ffn2_w13, ffn2_w2, w_in, w_out, fox_fb,
           mlstm_conv_w, mlstm_conv_b, mlstm_ib, mlstm_fb, mlstm_norm_g, ret_norm_g, ssd_conv_w,
           ssd_conv_b, ssd_dt_bias, ssd_A_log, ssd_D, ssd_norm_g, final_g):
    n_layers = ada_w.shape[0]
    b, s, d = x.shape
    hd = HEAD_DIM

    cond = _adaln(c, ada_w, ada_b).reshape(n_layers * b * N_MOD, 1, d)

    inv = 1.0 / (ROPE_BASE ** (jnp.arange(0, hd, 2, dtype=F32) / hd))
    inv_row = jnp.tile(inv, 2 * N_HEADS).reshape(1, GROUP_WIDTH)
    sgn_row = jnp.tile(jnp.concatenate([-jnp.ones(hd // 2, F32), jnp.ones(hd // 2, F32)]),
                       N_HEADS).reshape(1, GROUP_WIDTH)
    log_gamma = jnp.log(1.0 - 2.0 ** (-RET_DECAY_OFFSET - jnp.arange(N_HEADS, dtype=F32)))
    lg_row = _lane_pad(log_gamma).reshape(1, LANES)

    w13_a, w2_a = ffn1_w13.astype(BF16), ffn1_w2.astype(BF16)
    w13_b, w2_b = ffn2_w13.astype(BF16), ffn2_w2.astype(BF16)
    w_out_b = w_out.astype(BF16)
    w_all, w_t = _inproj_weights(w_in)

    for l in range(n_layers):
        def cond_base(bb, l=l):
            return (l * b + bb) * N_MOD

        x = _ffn(x, cond, cond_base, 0, norm_g[l, 0], w13_a, w2_a, l, final_g, False)
        pb, pm, pr, ps, pt, gr = _inproj(x, cond, cond_base, norm_g[l, 1], w_all, w_t, l)

        gate_bias = jnp.concatenate([fox_fb[l], mlstm_ib[l], mlstm_fb[l], ssd_dt_bias[l]])
        bias_col = jnp.broadcast_to(gate_bias[:, None], (GATE_ROWS, LANES))
        a_vec = jnp.zeros((GATE_ROWS,), F32).at[G_DT:G_DT + N_HEADS].set(
            -jnp.exp(ssd_A_log[l].astype(F32)))
        a_col = jnp.broadcast_to(a_vec[:, None], (GATE_ROWS, LANES))
        d_row = jnp.repeat(ssd_D[l], hd).reshape(1, GROUP_WIDTH)

        y_fox = _fox(pb, pt, gr, bias_col)
        y_mlstm = _mlstm(pm, pb, gr, mlstm_conv_w[l], mlstm_conv_b[l].reshape(1, -1),
                         bias_col, mlstm_norm_g[l])
        y_ret = _ret(pr, pb, lg_row, inv_row, sgn_row, ret_norm_g[l])
        y_ssd = _ssd(ps, gr, ssd_conv_w[l], ssd_conv_b[l].reshape(1, -1), bias_col,
                     a_col, d_row, ssd_norm_g[l])

        x = _ffn(x, cond, cond_base, 6, norm_g[l, 2], w13_b, w2_b, l, final_g, l == n_layers - 1,
                 mixer_ys=(y_fox, y_mlstm, y_ret, y_ssd), w_out=w_out_b)
    return x
```

```python
import functools

import numpy as np
import jax
import jax.numpy as jnp
from jax import lax
from jax.experimental import pallas as pl
from jax.experimental.pallas import tpu as pltpu

F32 = jnp.float32
BF16 = jnp.bfloat16

N_HEADS = 4
HEAD_DIM = 64
GROUP_WIDTH = N_HEADS * HEAD_DIM
CONV_K = 4
SSD_GROUPS = 2
N_MOD = 9
ROPE_BASE = 10000.0
RET_DECAY_OFFSET = 5.0
RMS_EPS = 1e-6
NEG_BIG = -1e30

VMEM_LIMIT_BYTES = 56 * 1024 * 1024
LANES = 128
GATE_ROWS = 16

G_FOX, G_MI, G_MF, G_DT = 0, 4, 8, 12

FFN_ROWS = 1024
FFN_SPLIT = 4
INPROJ_ROWS = 512
INPROJ_SPLIT = 2
ATT_BLOCK = 256
FOX_SUM_ROWS = 16
FOX_BATCH = 4
CHUNK = 128
MIX_BATCH = 4
MIX_ROWS = 512


def _cparams(n_axes):
    return pltpu.CompilerParams(dimension_semantics=("arbitrary",) * n_axes,
                                vmem_limit_bytes=VMEM_LIMIT_BYTES)


def _sigmoid(x):
    return 1.0 / (1.0 + jnp.exp(-x))


def _silu(x):
    return x * _sigmoid(x)


def _softplus(x):
    return jnp.maximum(x, 0.0) + jnp.log1p(jnp.exp(-jnp.abs(x)))


def _log_sigmoid(x):
    return -_softplus(-x)


def _rms_mod(x, g, sh, sc):
    y = x * lax.rsqrt(jnp.mean(x * x, axis=-1, keepdims=True) + RMS_EPS) * g
    return y * (1.0 + sc) + sh


def _dot(a, b):
    return jnp.dot(a, b, preferred_element_type=F32)


def _dot_nt(a, b):
    return lax.dot_general(a, b, (((1,), (1,)), ((), ())), preferred_element_type=F32)


def _dot_tn(a, b):
    return lax.dot_general(a, b, (((0,), (0,)), ((), ())), preferred_element_type=F32)


def _split3(x):
    hi = x.astype(BF16)
    r1 = x - hi.astype(F32)
    mid = r1.astype(BF16)
    lo = (r1 - mid.astype(F32)).astype(BF16)
    return hi, mid, lo


def _cumsum_rows(tri_lower, x):
    hi, mid, lo = _split3(x)
    return _dot(tri_lower, hi) + _dot(tri_lower, mid) + _dot(tri_lower, lo)


def _cumsum_lanes(x, tri_upper):
    return _dot(jnp.concatenate(_split3(x), axis=1),
                jnp.concatenate([tri_upper, tri_upper, tri_upper], axis=0))


def _tri(n):
    r = lax.broadcasted_iota(jnp.int32, (n, n), 0)
    c = lax.broadcasted_iota(jnp.int32, (n, n), 1)
    causal = r >= c
    lower = jnp.where(causal, 1.0, 0.0).astype(BF16)
    upper = jnp.where(r <= c, 1.0, 0.0).astype(BF16)
    return causal, lower, upper


def _head_of_lane(width=GROUP_WIDTH):
    return lax.broadcasted_iota(jnp.int32, (1, width), 1) // HEAD_DIM


def _head_masks(head_lane):
    return [jnp.where(head_lane == h, 1.0, 0.0).astype(BF16) for h in range(N_HEADS)]


def _expand(vals, head_lane):
    out = vals[N_HEADS - 1]
    for h in range(N_HEADS - 2, -1, -1):
        out = jnp.where(head_lane == h, vals[h], out)
    return out


def _head_mean_sq(y, gmat):
    return _dot(jnp.concatenate(_split3(y * y), axis=1), jnp.concatenate([gmat, gmat, gmat], axis=0))


def _group_mean_matrix():
    r = lax.broadcasted_iota(jnp.int32, (GROUP_WIDTH, GROUP_WIDTH), 0) // HEAD_DIM
    c = lax.broadcasted_iota(jnp.int32, (GROUP_WIDTH, GROUP_WIDTH), 1) // HEAD_DIM
    return jnp.where(r == c, 1.0 / HEAD_DIM, 0.0).astype(BF16)


def _block_diag_mask(cols):
    r = lax.broadcasted_iota(jnp.int32, (GROUP_WIDTH, cols), 0) // HEAD_DIM
    c = lax.broadcasted_iota(jnp.int32, (GROUP_WIDTH, cols), 1)
    same = (c < GROUP_WIDTH) & (r == c // HEAD_DIM)
    if cols > GROUP_WIDTH:
        same = same | ((c >= GROUP_WIDTH) & (r == c - GROUP_WIDTH))
    return same


def _causal_conv_silu(x, prev_tail, w_ref, b_ref):
    n = x.shape[0]
    row8 = lax.broadcasted_iota(jnp.int32, (8, 1), 0)
    x8 = x[0:8, :]
    acc = x * w_ref[CONV_K - 1:CONV_K, :] + b_ref[...]
    acc8 = acc[0:8, :]
    for j in range(1, CONV_K):
        wj = w_ref[CONV_K - 1 - j:CONV_K - j, :]
        acc = acc + pltpu.roll(x, j, 0) * wj
        acc8 = acc8 + jnp.where(row8 < j, pltpu.roll(prev_tail, j, 0), pltpu.roll(x8, j, 0)) * wj
    return _silu(jnp.concatenate([acc8, acc[8:n, :]], axis=0))


def _adaln_body(c_ref, w_ref, b_ref, o_ref):
    c = c_ref[...]
    o_ref[0] = _dot(_silu(c).astype(BF16), w_ref[0].astype(BF16)) + b_ref[0]


def _adaln(c, ada_w, ada_b):
    n_layers, d, n_out = ada_w.shape
    b = c.shape[0]
    tn = 1152
    return pl.pallas_call(
        _adaln_body,
        grid=(n_layers, n_out // tn),
        in_specs=[pl.BlockSpec((b, d), lambda l, j: (0, 0)),
                  pl.BlockSpec((1, d, tn), lambda l, j: (l, 0, j)),
                  pl.BlockSpec((1, 1, tn), lambda l, j: (l, 0, j))],
        out_specs=pl.BlockSpec((1, b, tn), lambda l, j: (l, 0, j)),
        out_shape=jax.ShapeDtypeStruct((n_layers, b, n_out), F32),
        compiler_params=_cparams(2),
        name="adaln",
    )(c, ada_w, ada_b.reshape(n_layers, 1, n_out))


def _cond_spec(cond_row, d):
    return pl.BlockSpec((1, 1, d), lambda b, i: (cond_row(b), 0, 0))


def _ffn_body(x_ref, sh_ref, sc_ref, gt_ref, ng_ref, w1_ref, w3_ref, w2_ref, fg_ref, *rest,
              apply_final, mixer_parts):
    o_ref = rest[-1]
    tm = x_ref.shape[1]
    sub = tm // FFN_SPLIT

    def row_group(r):
        rs = slice(r * sub, (r + 1) * sub)
        x = x_ref[0, rs, :]
        if mixer_parts:
            mg_ref, wo_ref = rest[0], rest[1 + mixer_parts]
            w = wo_ref.shape[0] // mixer_parts
            proj = _dot(rest[1][0, rs, :], wo_ref[0:w, :])
            for part in range(1, mixer_parts):
                proj += _dot(rest[1 + part][0, rs, :], wo_ref[part * w:(part + 1) * w, :])
            yield
            x = x + mg_ref[0] * proj
        hb = _rms_mod(x, ng_ref[...], sh_ref[0], sc_ref[0]).astype(BF16)
        yield
        a = _dot(hb, w1_ref[...])
        g = _dot(hb, w3_ref[...])
        yield
        act = (_silu(g) * a).astype(BF16)
        yield
        down = _dot(act, w2_ref[...])
        yield
        y = x + (0.5 * gt_ref[0]) * down
        if apply_final:
            y = y * lax.rsqrt(jnp.mean(y * y, axis=-1, keepdims=True) + RMS_EPS) * fg_ref[...]
        o_ref[0, rs, :] = y

    _lockstep(row_group(r) for r in range(FFN_SPLIT))


def _ffn(x, cond, cond_base, mod0, norm_g, w13, w2, layer, final_g, apply_final,
         mixer_ys=(), w_out=None):
    b, s, d = x.shape
    ff = w2.shape[1]
    tm = FFN_ROWS
    single = pl.Buffered(1)
    in_specs = [pl.BlockSpec((1, tm, d), lambda bb, i: (bb, i, 0)),
                _cond_spec(lambda bb: cond_base(bb) + mod0, d),
                _cond_spec(lambda bb: cond_base(bb) + mod0 + 1, d),
                _cond_spec(lambda bb: cond_base(bb) + mod0 + 2, d),
                pl.BlockSpec((1, d), lambda bb, i: (0, 0)),
                pl.BlockSpec((None, d, ff), lambda bb, i: (layer, 0, 0), pipeline_mode=single),
                pl.BlockSpec((None, d, ff), lambda bb, i: (layer, 0, 1), pipeline_mode=single),
                pl.BlockSpec((None, ff, d), lambda bb, i: (layer, 0, 0), pipeline_mode=single),
                pl.BlockSpec((1, d), lambda bb, i: (0, 0))]
    args = [x, cond, cond, cond, norm_g.reshape(1, d), w13, w13, w2, final_g.reshape(1, d)]
    if mixer_ys:
        in_specs.append(_cond_spec(lambda bb: cond_base(bb) + mod0 - 1, d))
        in_specs += [pl.BlockSpec((1, tm, y.shape[2]), lambda bb, i: (bb, i, 0)) for y in mixer_ys]
        in_specs.append(pl.BlockSpec((None,) + w_out.shape[1:], lambda bb, i: (layer, 0, 0),
                                     pipeline_mode=single))
        args += [cond, *mixer_ys, w_out]
    return pl.pallas_call(
        functools.partial(_ffn_body, apply_final=apply_final, mixer_parts=len(mixer_ys)),
        grid=(b, s // tm),
        in_specs=in_specs,
        out_specs=pl.BlockSpec((1, tm, d), lambda bb, i: (bb, i, 0)),
        out_shape=jax.ShapeDtypeStruct((b, s, d), F32),
        compiler_params=_cparams(2),
        name="ffn",
    )(*args)


N_PB = 3 * GROUP_WIDTH
N_PM = 3 * GROUP_WIDTH
N_PR = 5 * GROUP_WIDTH
N_PS = 3 * GROUP_WIDTH
N_PROJ = N_PB + N_PM + N_PR + N_PS
N_PT = 2 * GROUP_WIDTH


def _inproj_body(x_ref, sh_ref, sc_ref, ng_ref, w_ref, wt_ref,
                 pb_ref, pm_ref, pr_ref, ps_ref, pt_ref, gr_ref):
    sub = x_ref.shape[1] // INPROJ_SPLIT

    def row_group(r):
        rs = slice(r * sub, (r + 1) * sub)
        hb = _rms_mod(x_ref[0, rs, :], ng_ref[...], sh_ref[0], sc_ref[0]).astype(BF16)
        yield
        p = _dot(hb, w_ref[...])
        pt = _dot_nt(wt_ref[...], hb)
        yield
        o = 0
        pb_ref[0, rs, :] = p[:, o:o + N_PB].astype(BF16)
        o += N_PB
        pm_ref[0, rs, :] = p[:, o:o + N_PM]
        o += N_PM
        pr_ref[0, rs, :] = p[:, o:o + N_PR]
        o += N_PR
        ps_ref[0, rs, :] = p[:, o:o + N_PS]
        pt_ref[0, :, rs] = pt[0:N_PT, :].astype(BF16)
        gr_ref[0, :, rs] = pt[N_PT:N_PT + GATE_ROWS, :]

    _lockstep(row_group(r) for r in range(INPROJ_SPLIT))


def _inproj(x, cond, cond_base, norm_g, w_all, w_t, layer):
    b, s, d = x.shape
    tm = INPROJ_ROWS
    single = pl.Buffered(1)

    def tok(width):
        return pl.BlockSpec((1, tm, width), lambda bb, i: (bb, i, 0))

    return pl.pallas_call(
        _inproj_body,
        grid=(b, s // tm),
        in_specs=[tok(d),
                  _cond_spec(lambda bb: cond_base(bb) + 3, d),
                  _cond_spec(lambda bb: cond_base(bb) + 4, d),
                  pl.BlockSpec((1, d), lambda bb, i: (0, 0)),
                  pl.BlockSpec((None, d, N_PROJ), lambda bb, i: (layer, 0, 0), pipeline_mode=single),
                  pl.BlockSpec((None, N_PT + GATE_ROWS, d), lambda bb, i: (layer, 0, 0))],
        out_specs=[tok(N_PB), tok(N_PM), tok(N_PR), tok(N_PS),
                   pl.BlockSpec((1, N_PT, tm), lambda bb, i: (bb, 0, i)),
                   pl.BlockSpec((1, GATE_ROWS, tm), lambda bb, i: (bb, 0, i))],
        out_shape=[jax.ShapeDtypeStruct((b, s, N_PB), BF16),
                   jax.ShapeDtypeStruct((b, s, N_PM), F32),
                   jax.ShapeDtypeStruct((b, s, N_PR), F32),
                   jax.ShapeDtypeStruct((b, s, N_PS), F32),
                   jax.ShapeDtypeStruct((b, N_PT, s), BF16),
                   jax.ShapeDtypeStruct((b, GATE_ROWS, s), F32)],
        compiler_params=_cparams(2),
        name="inproj",
    )(x, cond, cond, norm_g.reshape(1, d), w_all, w_t)


def _fox_body(k_ref, qt_ref, vt_ref, gr_ref, bcol_ref, y_ref, cum_c, cum_r, acc_s):
    nb, s_len = k_ref.shape[0], k_ref.shape[1]
    t = ATT_BLOCK
    w = GROUP_WIDTH
    nblk = s_len // t
    _, _, tri_u = _tri(t)
    key_le_query = (lax.broadcasted_iota(jnp.int32, (t, t), 0)
                    <= lax.broadcasted_iota(jnp.int32, (t, t), 1))
    row_head = lax.broadcasted_iota(jnp.int32, (w, t), 0) // HEAD_DIM
    rmask = [jnp.where(row_head == h, 1.0, 0.0).astype(BF16) for h in range(N_HEADS)]
    units = [(bi, h) for bi in range(nb) for h in range(N_HEADS)]
    acc_rows = acc_s.shape[2]
    ones_rows = jnp.ones((acc_rows - HEAD_DIM, t), BF16)

    def cum_step(c, carry):
        r0 = pl.multiple_of(c * t, t)
        out = []
        for bi in range(nb):
            lf_r = _log_sigmoid(gr_ref[bi, :, pl.ds(r0, t)] + bcol_ref[:, 0:1])
            cr = _cumsum_lanes(lf_r, tri_u) + carry[bi]
            cum_r[bi, :, pl.ds(r0, t)] = cr
            cum_c[bi, pl.ds(r0, t), :] = cr.T
            out.append(cr[:, t - 1:t])
        return tuple(out)

    lax.fori_loop(0, nblk, cum_step, tuple(jnp.zeros((GATE_ROWS, 1), F32) for _ in range(nb)))

    def q_step(i, _):
        q0 = pl.multiple_of(i * t, t)
        qts, cts = [], []
        for bi, h in units:
            qts.append(qt_ref[bi, :, pl.ds(q0, t)] * rmask[h])
            cts.append(cum_r[bi, G_FOX + h:G_FOX + h + 1, pl.ds(q0, t)])
        acc_s[...] = jnp.zeros((nb, N_HEADS, acc_rows, t), F32)

        def kv_block(j, maxes, diagonal):
            k0 = pl.multiple_of(j * t, t)
            kbs = [k_ref[bi, pl.ds(k0, t), :] for bi in range(nb)]
            scores = [_dot(kbs[bi], qts[u]) for u, (bi, h) in enumerate(units)]
            new_maxes, probs, alphas = [], [], []
            for u, (bi, h) in enumerate(units):
                cs = cum_c[bi, pl.ds(k0, t), G_FOX + h:G_FOX + h + 1]
                sc = scores[u] + cts[u] - cs
                if diagonal:
                    sc = jnp.where(key_le_query, sc, NEG_BIG)
                m_new = jnp.maximum(maxes[u], jnp.max(sc, axis=0, keepdims=True))
                alphas.append(jnp.exp(maxes[u] - m_new))
                probs.append(jnp.exp(sc - m_new).astype(BF16))
                new_maxes.append(m_new)
            for u, (bi, h) in enumerate(units):
                v_aug = jnp.concatenate(
                    [vt_ref[bi, h * HEAD_DIM:(h + 1) * HEAD_DIM, pl.ds(k0, t)], ones_rows], axis=0)
                acc_s[bi, h] = alphas[u] * acc_s[bi, h] + _dot(v_aug, probs[u])
            return tuple(new_maxes)

        init = tuple(jnp.full((1, t), NEG_BIG, F32) for _ in units)
        maxes = lax.fori_loop(0, i, lambda j, mx: kv_block(j, mx, False), init)
        kv_block(i, maxes, True)
        for bi in range(nb):
            out_t = jnp.concatenate(
                [acc_s[bi, h, 0:HEAD_DIM, :] / acc_s[bi, h, HEAD_DIM:HEAD_DIM + 1, :]
                 for h in range(N_HEADS)], axis=0)
            y_ref[bi, pl.ds(q0, t), :] = out_t.T.astype(BF16)
        return 0

    lax.fori_loop(0, nblk, q_step, 0)


def _fox(pb, pt, gr, bias_col):
    b, s, _ = pb.shape
    w = GROUP_WIDTH
    t = ATT_BLOCK
    nb = FOX_BATCH
    return pl.pallas_call(
        _fox_body,
        grid=(b // nb,),
        in_specs=[pl.BlockSpec((nb, s, w), lambda bb: (bb, 0, 0)),
                  pl.BlockSpec((nb, w, s), lambda bb: (bb, 0, 0)),
                  pl.BlockSpec((nb, w, s), lambda bb: (bb, 1, 0)),
                  pl.BlockSpec((nb, GATE_ROWS, s), lambda bb: (bb, 0, 0)),
                  pl.BlockSpec((GATE_ROWS, LANES), lambda bb: (0, 0))],
        out_specs=pl.BlockSpec((nb, s, w), lambda bb: (bb, 0, 0)),
        out_shape=jax.ShapeDtypeStruct((b, s, w), BF16),
        scratch_shapes=[pltpu.VMEM((nb, s, GATE_ROWS), F32), pltpu.VMEM((nb, GATE_ROWS, s), F32),
                        pltpu.VMEM((nb, N_HEADS, HEAD_DIM + FOX_SUM_ROWS, t), F32)],
        compiler_params=_cparams(1),
        name="fox",
    )(pb, pt, pt, gr, bias_col)


def _lockstep(gens):
    gens = list(gens)
    while gens:
        alive = []
        for g in gens:
            try:
                next(g)
                alive.append(g)
            except StopIteration:
                pass
        gens = alive


def _ret_body(p_ref, v_ref, lg_ref, inv_ref, sgn_ref, ng_ref, y_ref, cos_s, sin_s, state_s):
    nb, ts = p_ref.shape[0], p_ref.shape[1]
    n = CHUNK
    w = GROUP_WIDTH
    si = pl.program_id(1)
    base = pl.multiple_of(si * ts, ts)
    causal, _, _ = _tri(n)
    head_lane = _head_of_lane()
    hmask = _head_masks(head_lane)
    gmat = _group_mean_matrix()
    bd = _block_diag_mask(w)

    @pl.when(pl.program_id(0) == 0)
    def _():
        def tab(c, _):
            r0 = pl.multiple_of(base + c * n, n)
            pos = (lax.broadcasted_iota(jnp.int32, (n, 1), 0) + r0).astype(F32)
            ang = pos * inv_ref[...]
            cos_s[pl.ds(r0, n), :] = jnp.cos(ang)
            sin_s[pl.ds(r0, n), :] = jnp.sin(ang) * sgn_ref[...]
            return 0
        lax.fori_loop(0, ts // n, tab, 0)

    @pl.when(si == 0)
    def _():
        state_s[...] = jnp.zeros((nb, w, w), F32)

    tcol = (lax.broadcasted_iota(jnp.int32, (n, 1), 0) + 1).astype(F32)
    trow = (lax.broadcasted_iota(jnp.int32, (1, n), 1) + 1).astype(F32)
    lgs = [lg_ref[0:1, h:h + 1] for h in range(N_HEADS)]
    decays = [jnp.where(causal, jnp.exp(tcol * lgs[h] - trow * lgs[h]), 0.0) for h in range(N_HEADS)]
    q_scale = _expand([jnp.exp(tcol * lgs[h]) for h in range(N_HEADS)], head_lane)
    w_end = _expand([jnp.exp(n * lgs[h] - tcol * lgs[h]) for h in range(N_HEADS)], head_lane)
    a_chunk = _expand([jnp.exp(n * lgs[h]) for h in range(N_HEADS)], head_lane)

    def per_batch(bi, c):
        r0 = pl.multiple_of(c * n, n)
        rows = pl.ds(r0, n)
        trows = pl.ds(pl.multiple_of(base + r0, n), n)
        cos = cos_s[trows, :]
        sin = sin_s[trows, :]
        q = p_ref[bi, rows, 0:w] * cos + p_ref[bi, rows, w:2 * w] * sin
        k = p_ref[bi, rows, 2 * w:3 * w] * cos + p_ref[bi, rows, 3 * w:4 * w] * sin
        v = v_ref[bi, rows, :]
        qb = q.astype(BF16)
        kb = k.astype(BF16)
        qs = _dot(qb, state_s[bi].astype(BF16))
        qk = _dot_nt(jnp.concatenate([qb * hmask[h] for h in range(N_HEADS)], axis=0), kb)
        yield
        probs = [(qk[h * n:(h + 1) * n, :] * decays[h]).astype(BF16) for h in range(N_HEADS)]
        pv = _dot(jnp.concatenate(probs, axis=1),
                  jnp.concatenate([v * hmask[h] for h in range(N_HEADS)], axis=0))
        kv = _dot_tn((k * w_end).astype(BF16), v)
        yield
        y = qs * q_scale + pv
        state_s[bi] = state_s[bi] * a_chunk + jnp.where(bd, kv, 0.0)
        msq = _head_mean_sq(y, gmat)
        yield
        yn = y * lax.rsqrt(msq + RMS_EPS) * ng_ref[...]
        y_ref[bi, rows, :] = (_silu(p_ref[bi, rows, 4 * w:5 * w]) * yn).astype(BF16)

    def chunk(c, _):
        _lockstep(per_batch(bi, c) for bi in range(nb))
        return 0

    lax.fori_loop(0, ts // n, chunk, 0)


def _mixer_specs(widths_and_blocks):
    return [pl.BlockSpec((MIX_BATCH, MIX_ROWS, width), lambda bb, ss, blk=blk: (bb, ss, blk))
            for width, blk in widths_and_blocks]


def _mixer_const(shape):
    return pl.BlockSpec(shape, lambda bb, ss: (0,) * len(shape))


def _ret(pr, pb, lg_row, inv_row, sgn_row, norm_g):
    b, s, _ = pr.shape
    w = GROUP_WIDTH
    return pl.pallas_call(
        _ret_body,
        grid=(b // MIX_BATCH, s // MIX_ROWS),
        in_specs=_mixer_specs([(N_PR, 0), (w, 2)]) + [
            _mixer_const((1, LANES)), _mixer_const((1, w)), _mixer_const((1, w)),
            _mixer_const((1, w))],
        out_specs=_mixer_specs([(w, 0)])[0],
        out_shape=jax.ShapeDtypeStruct((b, s, w), BF16),
        scratch_shapes=[pltpu.VMEM((s, w), F32), pltpu.VMEM((s, w), F32),
                        pltpu.VMEM((MIX_BATCH, w, w), F32)],
        compiler_params=_cparams(2),
        name="retention",
    )(pr, pb, lg_row, inv_row, sgn_row, norm_g.reshape(1, w))


def _ssd_body(p_ref, gr_ref, cw_ref, cb_ref, bcol_ref, acol_ref, dskip_ref, ng_ref, y_ref,
              state_s, tail_s, x_s, q_s, k_s, v_s, vw_s, y_s, crow_s, ccol_s, ctmat_s, qscale_s,
              dtl_s, wend_s):
    nb, ts = p_ref.shape[0], p_ref.shape[1]
    n = CHUNK
    w = GROUP_WIDTH
    causal, _, _ = _tri(n)
    head_lane = _head_of_lane()
    hmask = _head_masks(head_lane)
    gw = SSD_GROUPS * HEAD_DIM
    hpg = N_HEADS // SSD_GROUPS
    group_lane = lax.broadcasted_iota(jnp.int32, (1, gw), 1) // HEAD_DIM
    gmask = [jnp.where(group_lane == g, 1.0, 0.0).astype(BF16) for g in range(SSD_GROUPS)]
    state_mask = (lax.broadcasted_iota(jnp.int32, (gw, w), 0) // HEAD_DIM
                  == lax.broadcasted_iota(jnp.int32, (gw, w), 1) // (HEAD_DIM * hpg))

    @pl.when(pl.program_id(1) == 0)
    def _():
        state_s[...] = jnp.zeros((nb, gw, w), F32)
        tail_s[...] = jnp.zeros((nb, 8, 2 * w), F32)

    batch = range(nb)
    n_chunks = ts // n
    half = G_DT - 8
    sel_row = lax.broadcasted_iota(jnp.int32, (16, 1), 0) - half
    sel_chunk = jnp.where(
        sel_row == lax.broadcasted_iota(jnp.int32, (1, N_HEADS * n), 1) // n, 1.0, 0.0).astype(BF16)
    sel_head = jnp.where(sel_row == head_lane, 1.0, 0.0).astype(BF16)
    r_i = lax.broadcasted_iota(jnp.int32, (n, 2 * n), 0)
    c_i = lax.broadcasted_iota(jnp.int32, (n, 2 * n), 1)
    scan_mat = jnp.where(((c_i < n) & (r_i <= c_i)) | ((c_i >= n) & (r_i > c_i - n)), 1.0, 0.0
                         ).astype(BF16)
    dt16s = [_softplus(gr_ref[bi] + bcol_ref[:, 0:1]) for bi in batch]
    la_stack = jnp.concatenate([(dt16s[bi] * acol_ref[:, 0:1])[:, c * n:(c + 1) * n]
                                for bi in batch for c in range(n_chunks)], axis=0)
    scans = _cumsum_lanes(la_stack, scan_mat)

    def unstack(bi, lo):
        return jnp.concatenate(
            [scans[(bi * n_chunks + c) * GATE_ROWS + 8:(bi * n_chunks + c + 1) * GATE_ROWS, lo:lo + n]
             for c in range(n_chunks)], axis=1)

    cum8s = [unstack(bi, 0) for bi in batch]
    rev8s = [unstack(bi, n) for bi in batch]
    dt8s = [dt16[8:16, :] for dt16 in dt16s]
    ct_mats = [_rows_to_lanes(c8, sel_chunk) for c8 in cum8s]
    q_scales = [jnp.exp(_rows_to_lanes(c8, sel_head)) for c8 in cum8s]
    dt_lanes = [_rows_to_lanes(d8, sel_head) for d8 in dt8s]
    w_ends = [jnp.exp(_rows_to_lanes(r8, sel_head)) for r8 in rev8s]
    for bi in batch:
        crow_s[bi] = cum8s[bi]
        ccol_s[bi] = cum8s[bi].T
        for h in range(N_HEADS):
            ctmat_s[bi, h] = ct_mats[bi][:, h * n:(h + 1) * n]
        qscale_s[bi] = q_scales[bi]
        dtl_s[bi] = dt_lanes[bi]
        wend_s[bi] = w_ends[bi]

    def conv_step(c, _):
        rows = pl.ds(pl.multiple_of(c * n, n), n)
        for bi in batch:
            pre = p_ref[bi, rows, w:3 * w]
            xbc = _causal_conv_silu(pre, tail_s[bi], cw_ref, cb_ref)
            tail_s[bi] = pre[n - 8:n, :]
            x = xbc[:, 0:w]
            x_s[bi, rows, :] = x
            k_s[bi, rows, :] = xbc[:, w:w + gw].astype(BF16)
            q_s[bi, rows, :] = xbc[:, w + gw:w + 2 * gw].astype(BF16)
            v = x * dtl_s[bi, rows, :]
            v_s[bi, rows, :] = v.astype(BF16)
            vw_s[bi, rows, :] = (v * wend_s[bi, rows, :]).astype(BF16)
        return 0

    lax.fori_loop(0, n_chunks, conv_step, 0)

    def per_batch(bi, c):
        rows = pl.ds(pl.multiple_of(c * n, n), n)
        qb = q_s[bi, rows, :]
        kb = k_s[bi, rows, :]
        vb = v_s[bi, rows, :]
        qs = _dot(qb, state_s[bi].astype(BF16))
        qk = _dot_nt(jnp.concatenate([qb * gmask[g] for g in range(SSD_GROUPS)], axis=0), kb)
        kv = _dot_tn(kb, vw_s[bi, rows, :])
        yield
        c_rows = crow_s[bi, :, rows]
        probs = [(qk[(h // hpg) * n:(h // hpg + 1) * n, :]
                  * jnp.where(causal, jnp.exp(ctmat_s[bi, h, rows, :]
                                              - c_rows[half + h:half + h + 1, :]), 0.0)
                  ).astype(BF16) for h in range(N_HEADS)]
        pv = _dot(jnp.concatenate(probs, axis=1),
                  jnp.concatenate([vb * hmask[h] for h in range(N_HEADS)], axis=0))
        yield
        y_s[bi, rows, :] = qs * qscale_s[bi, rows, :] + pv
        c_end = ccol_s[bi, pl.ds(pl.multiple_of(c * n, n) + n - 8, 8), :][7:8, :]
        a_chunk = _expand([jnp.exp(c_end[:, half + h:half + h + 1]) for h in range(N_HEADS)],
                          head_lane)
        state_s[bi] = state_s[bi] * a_chunk + jnp.where(state_mask, kv, 0.0)

    def chunk(c, _):
        _lockstep(per_batch(bi, c) for bi in batch)
        return 0

    lax.fori_loop(0, n_chunks, chunk, 0)

    def norm_step(c, _):
        rows = pl.ds(pl.multiple_of(c * n, n), n)
        for bi in batch:
            hs = (y_s[bi, rows, :] + dskip_ref[...] * x_s[bi, rows, :]) * _silu(p_ref[bi, rows, 0:w])
            yn = hs * lax.rsqrt(jnp.mean(hs * hs, axis=-1, keepdims=True) + RMS_EPS) * ng_ref[...]
            y_ref[bi, rows, :] = yn.astype(BF16)
        return 0

    lax.fori_loop(0, n_chunks, norm_step, 0)


def _gate_row_spec():
    return pl.BlockSpec((MIX_BATCH, GATE_ROWS, MIX_ROWS), lambda bb, ss: (bb, 0, ss))


def _ssd(ps, gr, conv_w, conv_b, bias_col, a_col, d_row, norm_g):
    b, s, _ = ps.shape
    w = GROUP_WIDTH
    gw = SSD_GROUPS * HEAD_DIM
    nb = MIX_BATCH
    const = _mixer_const
    return pl.pallas_call(
        _ssd_body,
        grid=(b // MIX_BATCH, s // MIX_ROWS),
        in_specs=_mixer_specs([(N_PS, 0)]) + [
            _gate_row_spec(),
            const((CONV_K, 2 * w)), const((1, 2 * w)),
            const((GATE_ROWS, LANES)), const((GATE_ROWS, LANES)),
            const((1, w)), const((1, w))],
        out_specs=_mixer_specs([(w, 0)])[0],
        out_shape=jax.ShapeDtypeStruct((b, s, w), BF16),
        scratch_shapes=[pltpu.VMEM((nb, gw, w), F32),
                        pltpu.VMEM((nb, 8, 2 * w), F32),
                        pltpu.VMEM((nb, MIX_ROWS, w), F32),
                        pltpu.VMEM((nb, MIX_ROWS, gw), BF16),
                        pltpu.VMEM((nb, MIX_ROWS, gw), BF16),
                        pltpu.VMEM((nb, MIX_ROWS, w), BF16),
                        pltpu.VMEM((nb, MIX_ROWS, w), BF16),
                        pltpu.VMEM((nb, MIX_ROWS, w), F32),
                        pltpu.VMEM((nb, 8, MIX_ROWS), F32),
                        pltpu.VMEM((nb, MIX_ROWS, 8), F32),
                        pltpu.VMEM((nb, N_HEADS, MIX_ROWS, CHUNK), F32),
                        pltpu.VMEM((nb, MIX_ROWS, w), F32),
                        pltpu.VMEM((nb, MIX_ROWS, w), F32),
                        pltpu.VMEM((nb, MIX_ROWS, w), F32)],
        compiler_params=_cparams(2),
        name="ssd",
    )(ps, gr, conv_w, conv_b, bias_col, a_col, d_row, norm_g.reshape(1, w))


def _rows_to_lanes(x8, sel):
    parts = _split3(jnp.concatenate([x8, jnp.zeros_like(x8)], axis=0))
    return _dot_tn(jnp.concatenate(parts, axis=0), jnp.concatenate([sel, sel, sel], axis=0))


def _chunk_cummax(xs, n):
    lane_in_chunk = lax.broadcasted_iota(jnp.int32, (1, xs[0].shape[1]), 1) % n
    shift = 1
    while shift < n:
        keep = lane_in_chunk >= shift
        xs = [jnp.maximum(x, jnp.where(keep, pltpu.roll(x, shift, 1), NEG_BIG)) for x in xs]
        shift *= 2
    return xs


def _mlstm_body(p_ref, v_ref, gr_ref, cw_ref, cb_ref, bcol_ref, ng_ref, y_ref,
                state_s, m_s, tail_s, q_s, k_s, h_s, rrow_s, colb_s, colc_s, cmat_s, cexp_s,
                rexp_s):
    nb, ts = p_ref.shape[0], p_ref.shape[1]
    n = CHUNK
    w = GROUP_WIDTH
    wa = w + LANES
    causal, _, _ = _tri(n)
    head_lane = _head_of_lane()
    hmask = _head_masks(head_lane)
    head_lane_aug = jnp.concatenate(
        [head_lane, lax.broadcasted_iota(jnp.int32, (1, LANES), 1)], axis=1)
    gmat = _group_mean_matrix()
    bd_aug = _block_diag_mask(wa)
    lane128 = lax.broadcasted_iota(jnp.int32, (1, LANES), 1)
    ones_col = [jnp.where(lane128 == h, 1.0, 0.0).astype(BF16) for h in range(N_HEADS)]

    @pl.when(pl.program_id(1) == 0)
    def _():
        state_s[...] = jnp.zeros((nb, w, wa), F32)
        m_s[...] = jnp.zeros((nb, 1, 8), F32)
        tail_s[...] = jnp.zeros((nb, 8, 2 * w), F32)

    _, _, tri_u = _tri(n)
    sel_row = lax.broadcasted_iota(jnp.int32, (16, 1), 0)
    sel_chunk = jnp.where(
        sel_row == lax.broadcasted_iota(jnp.int32, (1, N_HEADS * n), 1) // n, 1.0, 0.0).astype(BF16)
    sel_head = jnp.where(sel_row == head_lane, 1.0, 0.0).astype(BF16)
    batch = range(nb)
    n_chunks = ts // n
    g_blks = [gr_ref[bi] + bcol_ref[:, 0:1] for bi in batch]
    lf_blks = [_log_sigmoid(g) for g in g_blks]
    b_stack = _cumsum_lanes(
        jnp.concatenate([lf[:, c * n:(c + 1) * n] for lf in lf_blks for c in range(n_chunks)], axis=0),
        tri_u)
    b16s = [jnp.concatenate([b_stack[(bi * n_chunks + c) * GATE_ROWS:(bi * n_chunks + c + 1) * GATE_ROWS]
                             for c in range(n_chunks)], axis=1) for bi in batch]
    r16s = [pltpu.roll(g_blks[bi], G_MF - G_MI, 0) - b16s[bi] for bi in batch]
    b8s = [b16[G_MF:G_MF + 8, :] for b16 in b16s]
    r8s = [r16[G_MF:G_MF + 8, :] for r16 in r16s]
    c8s = _chunk_cummax(r8s, n)
    c_mats = [_rows_to_lanes(c8, sel_chunk) for c8 in c8s]
    c_exps = [_rows_to_lanes(c8, sel_head) for c8 in c8s]
    r_exps = [_rows_to_lanes(r8, sel_head) for r8 in r8s]
    for bi in batch:
        rrow_s[bi] = r8s[bi]
        colb_s[bi] = b8s[bi].T
        colc_s[bi] = c8s[bi].T
        for h in range(N_HEADS):
            cmat_s[bi, h] = c_mats[bi][:, h * n:(h + 1) * n]
        cexp_s[bi] = c_exps[bi]
        rexp_s[bi] = r_exps[bi]
    def conv_step(c, _):
        rows = pl.ds(pl.multiple_of(c * n, n), n)
        for bi in batch:
            pre = p_ref[bi, rows, 0:2 * w]
            qk_act = _causal_conv_silu(pre, tail_s[bi], cw_ref, cb_ref)
            tail_s[bi] = pre[n - 8:n, :]
            q_s[bi, rows, :] = (qk_act[:, 0:w] * (HEAD_DIM ** -0.5)).astype(BF16)
            k_s[bi, rows, :] = qk_act[:, w:2 * w]
        return 0

    lax.fori_loop(0, n_chunks, conv_step, 0)

    def per_batch(bi, c):
        r0 = pl.multiple_of(c * n, n)
        rows = pl.ds(r0, n)
        qb = q_s[bi, rows, :]
        k = k_s[bi, rows, :]
        kb = k.astype(BF16)
        vb = v_ref[bi, rows, :]

        qs = _dot(qb, state_s[bi].astype(BF16))
        qk = _dot_nt(jnp.concatenate([qb * hmask[h] for h in range(N_HEADS)], axis=0), kb)
        yield

        heads = range(N_HEADS)
        m8 = m_s[bi]
        m_ins = [m8[:, h:h + 1] for h in heads]
        r_rows = rrow_s[bi, :, rows]
        col_b = colb_s[bi, rows, :]
        col_c = colc_s[bi, rows, :]
        c_end8 = col_c[n - 1:n, :]
        z_end8 = jnp.maximum(m8, c_end8)
        probs = [(jnp.where(causal,
                            jnp.exp(r_rows[h:h + 1, :] - jnp.maximum(m_ins[h], cmat_s[bi, h, rows, :])),
                            0.0) * qk[h * n:(h + 1) * n, :]).astype(BF16) for h in heads]
        v_rows = jnp.concatenate(
            [jnp.concatenate([vb * hmask[h], jnp.broadcast_to(ones_col[h], (n, LANES))], axis=1)
             for h in heads], axis=0)
        pv = _dot(jnp.concatenate(probs, axis=1), v_rows)
        c_ends = [c_end8[:, h:h + 1] for h in heads]
        kw = (k * jnp.exp(rexp_s[bi, rows, :] - _expand(c_ends, head_lane))).astype(BF16)
        v_aug = jnp.concatenate([vb, jnp.ones((n, LANES), BF16)], axis=1)
        kv = _dot_tn(kw, v_aug)
        yield
        m_lanes = _expand(m_ins, head_lane)
        a_inter = jnp.exp(m_lanes - jnp.maximum(m_lanes, cexp_s[bi, rows, :]))
        num = pv[:, 0:w] + a_inter * qs[:, 0:w]
        z8 = jnp.maximum(m8, col_c)
        den8 = pv[:, w:w + 8] + jnp.exp(m8 - z8) * qs[:, w:w + 8]
        dens8 = jnp.maximum(jnp.abs(den8), jnp.exp(-(col_b + z8)))
        h_s[bi, rows, :] = num / _expand([dens8[:, h:h + 1] for h in heads], head_lane)
        a_old8 = jnp.exp(m8 - z_end8)
        a_loc8 = jnp.exp(c_end8 - z_end8)
        state_s[bi] = (state_s[bi] * _expand([a_old8[:, h:h + 1] for h in heads], head_lane_aug)
                       + jnp.where(bd_aug, kv, 0.0)
                       * _expand([a_loc8[:, h:h + 1] for h in heads], head_lane_aug))
        m_s[bi] = col_b[n - 1:n, :] + z_end8

    def chunk(c, _):
        _lockstep(per_batch(bi, c) for bi in range(nb))
        return 0

    lax.fori_loop(0, ts // n, chunk, 0)

    def norm_step(c, _):
        rows = pl.ds(pl.multiple_of(c * n, n), n)
        hvals = [h_s[bi, rows, :] for bi in batch]
        msqs = [_head_mean_sq(hv, gmat) for hv in hvals]
        for bi in batch:
            yn = hvals[bi] * lax.rsqrt(msqs[bi] + RMS_EPS) * ng_ref[...]
            y_ref[bi, rows, :] = (_sigmoid(p_ref[bi, rows, 2 * w:3 * w]) * yn).astype(BF16)
        return 0

    lax.fori_loop(0, n_chunks, norm_step, 0)


def _mlstm(pm, pb, gr, conv_w, conv_b, bias_col, norm_g):
    b, s, _ = pm.shape
    w = GROUP_WIDTH
    const = _mixer_const
    return pl.pallas_call(
        _mlstm_body,
        grid=(b // MIX_BATCH, s // MIX_ROWS),
        in_specs=_mixer_specs([(N_PM, 0), (w, 1)]) + [
            _gate_row_spec(),
            const((CONV_K, 2 * w)), const((1, 2 * w)),
            const((GATE_ROWS, LANES)), const((1, w))],
        out_specs=_mixer_specs([(w, 0)])[0],
        out_shape=jax.ShapeDtypeStruct((b, s, w), BF16),
        scratch_shapes=[pltpu.VMEM((MIX_BATCH, w, w + LANES), F32),
                        pltpu.VMEM((MIX_BATCH, 1, 8), F32),
                        pltpu.VMEM((MIX_BATCH, 8, 2 * w), F32),
                        pltpu.VMEM((MIX_BATCH, MIX_ROWS, w), BF16),
                        pltpu.VMEM((MIX_BATCH, MIX_ROWS, w), F32),
                        pltpu.VMEM((MIX_BATCH, MIX_ROWS, w), F32),
                        pltpu.VMEM((MIX_BATCH, 8, MIX_ROWS), F32),
                        pltpu.VMEM((MIX_BATCH, MIX_ROWS, 8), F32),
                        pltpu.VMEM((MIX_BATCH, MIX_ROWS, 8), F32),
                        pltpu.VMEM((MIX_BATCH, N_HEADS, MIX_ROWS, CHUNK), F32),
                        pltpu.VMEM((MIX_BATCH, MIX_ROWS, w), F32),
                        pltpu.VMEM((MIX_BATCH, MIX_ROWS, w), F32)],
        compiler_params=_cparams(2),
        name="mlstm",
    )(pm, pb, gr, conv_w, conv_b, bias_col, norm_g.reshape(1, w))


def _half_swap(t):
    lead = t.shape[:-1]
    t = t.reshape(lead + (N_HEADS, 2, HEAD_DIM // 2))
    return t[..., ::-1, :].reshape(lead + (GROUP_WIDTH,))


def _inproj_weights(w_in):
    w, h, hd = GROUP_WIDTH, N_HEADS, HEAD_DIM
    n_layers, d, _ = w_in.shape
    qk_scale = hd ** -0.5
    fox0 = 0
    ml0 = fox0 + 3 * w + h
    ret0 = ml0 + 4 * w + 2 * h
    ssd0 = ret0 + 4 * w
    bc = 2 * SSD_GROUPS * hd

    def col(start, width):
        return w_in[:, :, start:start + width]

    rq, rk = col(ret0, w), col(ret0 + w, w) * qk_scale
    gates = jnp.concatenate([col(fox0 + 3 * w, h), col(ml0 + 3 * w, 2 * h), col(ssd0 + 2 * w + bc, h)],
                            axis=2)
    w_all = jnp.concatenate([
        col(fox0 + w, w), col(ml0 + 2 * w, w), col(ret0 + 2 * w, w),
        col(ml0, 2 * w), col(ml0 + 3 * w + 2 * h, w),
        rq, _half_swap(rq), rk, _half_swap(rk), col(ret0 + 3 * w, w),
        col(ssd0, w), col(ssd0 + w, w + bc)], axis=2)
    w_t = jnp.concatenate([col(fox0, w) * qk_scale, col(fox0 + 2 * w, w), gates], axis=2)
    return w_all.astype(BF16), jnp.swapaxes(w_t, 1, 2).astype(BF16)


def _lane_pad(v, width=LANES):
    return jnp.zeros((width,), F32).at[:v.shape[0]].set(v)


def kernel(x, c, ada_w, ada_b, norm_g, ffn1_w13, ffn1_w2, ffn2_w13, ffn2_w2, w_in, w_out, fox_fb,
           mlstm_conv_w, mlstm_conv_b, mlstm_ib, mlstm_fb, mlstm_norm_g, ret_norm_g, ssd_conv_w,
           ssd_conv_b, ssd_dt_bias, ssd_A_log, ssd_D, ssd_norm_g, final_g):
    n_layers = ada_w.shape[0]
    b, s, d = x.shape
    hd = HEAD_DIM

    cond = _adaln(c, ada_w, ada_b).reshape(n_layers * b * N_MOD, 1, d)

    inv = 1.0 / (ROPE_BASE ** (jnp.arange(0, hd, 2, dtype=F32) / hd))
    inv_row = jnp.tile(inv, 2 * N_HEADS).reshape(1, GROUP_WIDTH)
    sgn_row = jnp.tile(jnp.concatenate([-jnp.ones(hd // 2, F32), jnp.ones(hd // 2, F32)]),
                       N_HEADS).reshape(1, GROUP_WIDTH)
    log_gamma = jnp.log(1.0 - 2.0 ** (-RET_DECAY_OFFSET - jnp.arange(N_HEADS, dtype=F32)))
    lg_row = _lane_pad(log_gamma).reshape(1, LANES)

    w13_a, w2_a = ffn1_w13.astype(BF16), ffn1_w2.astype(BF16)
    w13_b, w2_b = ffn2_w13.astype(BF16), ffn2_w2.astype(BF16)
    w_out_b = w_out.astype(BF16)
    w_all, w_t = _inproj_weights(w_in)

    for l in range(n_layers):
        def cond_base(bb, l=l):
            return (l * b + bb) * N_MOD

        x = _ffn(x, cond, cond_base, 0, norm_g[l, 0], w13_a, w2_a, l, final_g, False)
        pb, pm, pr, ps, pt, gr = _inproj(x, cond, cond_base, norm_g[l, 1], w_all, w_t, l)

        gate_bias = jnp.concatenate([fox_fb[l], mlstm_ib[l], mlstm_fb[l], ssd_dt_bias[l]])
        bias_col = jnp.broadcast_to(gate_bias[:, None], (GATE_ROWS, LANES))
        a_vec = jnp.zeros((GATE_ROWS,), F32).at[G_DT:G_DT + N_HEADS].set(
            -jnp.exp(ssd_A_log[l].astype(F32)))
        a_col = jnp.broadcast_to(a_vec[:, None], (GATE_ROWS, LANES))
        d_row = jnp.repeat(ssd_D[l], hd).reshape(1, GROUP_WIDTH)

        y_fox = _fox(pb, pt, gr, bias_col)
        y_mlstm = _mlstm(pm, pb, gr, mlstm_conv_w[l], mlstm_conv_b[l].reshape(1, -1),
                         bias_col, mlstm_norm_g[l])
        y_ret = _ret(pr, pb, lg_row, inv_row, sgn_row, ret_norm_g[l])
        y_ssd = _ssd(ps, gr, ssd_conv_w[l], ssd_conv_b[l].reshape(1, -1), bias_col,
                     a_col, d_row, ssd_norm_g[l])

        x = _ffn(x, cond, cond_base, 6, norm_g[l, 2], w13_b, w2_b, l, final_g, l == n_layers - 1,
                 mixer_ys=(y_fox, y_mlstm, y_ret, y_ssd), w_out=w_out_b)
    return x
```

```python
import functools

import numpy as np
import jax
import jax.numpy as jnp
from jax import lax
from jax.experimental import pallas as pl
from jax.experimental.pallas import tpu as pltpu

F32 = jnp.float32
BF16 = jnp.bfloat16

N_HEADS = 4
HEAD_DIM = 64
GROUP_WIDTH = N_HEADS * HEAD_DIM
CONV_K = 4
SSD_GROUPS = 2
N_MOD = 9
ROPE_BASE = 10000.0
RET_DECAY_OFFSET = 5.0
RMS_EPS = 1e-6
NEG_BIG = -1e30

VMEM_LIMIT_BYTES = 56 * 1024 * 1024
LANES = 128
GATE_ROWS = 16

G_FOX, G_MI, G_MF, G_DT = 0, 4, 8, 12

FFN_ROWS = 1024
FFN_SPLIT = 4
INPROJ_ROWS = 512
INPROJ_SPLIT = 2
ATT_BLOCK = 256
FOX_SUM_ROWS = 16
FOX_BATCH = 4
CHUNK = 128
MIX_BATCH = 4
MIX_ROWS = 512


def _cparams(n_axes):
    return pltpu.CompilerParams(dimension_semantics=("arbitrary",) * n_axes,
                                vmem_limit_bytes=VMEM_LIMIT_BYTES)


def _sigmoid(x):
    return 1.0 / (1.0 + jnp.exp(-x))


def _silu(x):
    return x * _sigmoid(x)


def _softplus(x):
    return jnp.maximum(x, 0.0) + jnp.log1p(jnp.exp(-jnp.abs(x)))


def _log_sigmoid(x):
    return -_softplus(-x)


def _rms_mod(x, g, sh, sc):
    y = x * lax.rsqrt(jnp.mean(x * x, axis=-1, keepdims=True) + RMS_EPS) * g
    return y * (1.0 + sc) + sh


def _dot(a, b):
    return jnp.dot(a, b, preferred_element_type=F32)


def _dot_nt(a, b):
    return lax.dot_general(a, b, (((1,), (1,)), ((), ())), preferred_element_type=F32)


def _dot_tn(a, b):
    return lax.dot_general(a, b, (((0,), (0,)), ((), ())), preferred_element_type=F32)


def _split3(x):
    hi = x.astype(BF16)
    r1 = x - hi.astype(F32)
    mid = r1.astype(BF16)
    lo = (r1 - mid.astype(F32)).astype(BF16)
    return hi, mid, lo


def _cumsum_rows(tri_lower, x):
    hi, mid, lo = _split3(x)
    return _dot(tri_lower, hi) + _dot(tri_lower, mid) + _dot(tri_lower, lo)


def _cumsum_lanes(x, tri_upper):
    return _dot(jnp.concatenate(_split3(x), axis=1),
                jnp.concatenate([tri_upper, tri_upper, tri_upper], axis=0))


def _tri(n):
    r = lax.broadcasted_iota(jnp.int32, (n, n), 0)
    c = lax.broadcasted_iota(jnp.int32, (n, n), 1)
    causal = r >= c
    lower = jnp.where(causal, 1.0, 0.0).astype(BF16)
    upper = jnp.where(r <= c, 1.0, 0.0).astype(BF16)
    return causal, lower, upper


def _head_of_lane(width=GROUP_WIDTH):
    return lax.broadcasted_iota(jnp.int32, (1, width), 1) // HEAD_DIM


def _head_masks(head_lane):
    return [jnp.where(head_lane == h, 1.0, 0.0).astype(BF16) for h in range(N_HEADS)]


def _expand(vals, head_lane):
    out = vals[N_HEADS - 1]
    for h in range(N_HEADS - 2, -1, -1):
        out = jnp.where(head_lane == h, vals[h], out)
    return out


def _head_mean_sq(y, gmat):
    return _dot(jnp.concatenate(_split3(y * y), axis=1), jnp.concatenate([gmat, gmat, gmat], axis=0))


def _group_mean_matrix():
    r = lax.broadcasted_iota(jnp.int32, (GROUP_WIDTH, GROUP_WIDTH), 0) // HEAD_DIM
    c = lax.broadcasted_iota(jnp.int32, (GROUP_WIDTH, GROUP_WIDTH), 1) // HEAD_DIM
    return jnp.where(r == c, 1.0 / HEAD_DIM, 0.0).astype(BF16)


def _block_diag_mask(cols):
    r = lax.broadcasted_iota(jnp.int32, (GROUP_WIDTH, cols), 0) // HEAD_DIM
    c = lax.broadcasted_iota(jnp.int32, (GROUP_WIDTH, cols), 1)
    same = (c < GROUP_WIDTH) & (r == c // HEAD_DIM)
    if cols > GROUP_WIDTH:
        same = same | ((c >= GROUP_WIDTH) & (r == c - GROUP_WIDTH))
    return same


def _causal_conv_silu(x, prev_tail, w_ref, b_ref):
    n = x.shape[0]
    row8 = lax.broadcasted_iota(jnp.int32, (8, 1), 0)
    x8 = x[0:8, :]
    acc = x * w_ref[CONV_K - 1:CONV_K, :] + b_ref[...]
    acc8 = acc[0:8, :]
    for j in range(1, CONV_K):
        wj = w_ref[CONV_K - 1 - j:CONV_K - j, :]
        acc = acc + pltpu.roll(x, j, 0) * wj
        acc8 = acc8 + jnp.where(row8 < j, pltpu.roll(prev_tail, j, 0), pltpu.roll(x8, j, 0)) * wj
    return _silu(jnp.concatenate([acc8, acc[8:n, :]], axis=0))


def _adaln_body(c_ref, w_ref, b_ref, o_ref):
    c = c_ref[...]
    o_ref[0] = _dot(_silu(c).astype(BF16), w_ref[0].astype(BF16)) + b_ref[0]


def _adaln(c, ada_w, ada_b):
    n_layers, d, n_out = ada_w.shape
    b = c.shape[0]
    tn = 1152
    return pl.pallas_call(
        _adaln_body,
        grid=(n_layers, n_out // tn),
        in_specs=[pl.BlockSpec((b, d), lambda l, j: (0, 0)),
                  pl.BlockSpec((1, d, tn), lambda l, j: (l, 0, j)),
                  pl.BlockSpec((1, 1, tn), lambda l, j: (l, 0, j))],
        out_specs=pl.BlockSpec((1, b, tn), lambda l, j: (l, 0, j)),
        out_shape=jax.ShapeDtypeStruct((n_layers, b, n_out), F32),
        compiler_params=_cparams(2),
        name="adaln",
    )(c, ada_w, ada_b.reshape(n_layers, 1, n_out))


def _layer_spec(stack, row):
    return pl.BlockSpec((None,) + stack.shape[1:], lambda *_: (row,) + (0,) * (stack.ndim - 1))


def _cond_spec(cond_row, d):
    return pl.BlockSpec((1, 1, d), lambda b, i: (cond_row(b), 0, 0))


def _ffn_body(x_ref, sh_ref, sc_ref, gt_ref, ng_ref, w1_ref, w3_ref, w2_ref, fg_ref, *rest,
              apply_final, mixer_parts):
    o_ref = rest[-1]
    tm = x_ref.shape[1]
    sub = tm // FFN_SPLIT

    def row_group(r):
        rs = slice(r * sub, (r + 1) * sub)
        x = x_ref[0, rs, :]
        if mixer_parts:
            mg_ref, wo_ref = rest[0], rest[1 + mixer_parts]
            w = wo_ref.shape[0] // mixer_parts
            proj = _dot(rest[1][0, rs, :], wo_ref[0:w, :])
            for part in range(1, mixer_parts):
                proj += _dot(rest[1 + part][0, rs, :], wo_ref[part * w:(part + 1) * w, :])
            yield
            x = x + mg_ref[0] * proj
        hb = _rms_mod(x, ng_ref[...], sh_ref[0], sc_ref[0]).astype(BF16)
        yield
        a = _dot(hb, w1_ref[...])
        g = _dot(hb, w3_ref[...])
        yield
        act = (_silu(g) * a).astype(BF16)
        yield
        down = _dot(act, w2_ref[...])
        yield
        y = x + (0.5 * gt_ref[0]) * down
        if apply_final:
            y = y * lax.rsqrt(jnp.mean(y * y, axis=-1, keepdims=True) + RMS_EPS) * fg_ref[...]
        o_ref[0, rs, :] = y

    _lockstep(row_group(r) for r in range(FFN_SPLIT))


def _ffn(x, cond, cond_base, mod0, norms, norm_row, w13, w2, layer, final_g, apply_final,
         mixer_ys=(), w_out=None):
    b, s, d = x.shape
    ff = w2.shape[1]
    tm = FFN_ROWS
    single = pl.Buffered(1)
    in_specs = [pl.BlockSpec((1, tm, d), lambda bb, i: (bb, i, 0)),
                _cond_spec(lambda bb: cond_base(bb) + mod0, d),
                _cond_spec(lambda bb: cond_base(bb) + mod0 + 1, d),
                _cond_spec(lambda bb: cond_base(bb) + mod0 + 2, d),
                _layer_spec(norms, norm_row),
                pl.BlockSpec((None, d, ff), lambda bb, i: (layer, 0, 0), pipeline_mode=single),
                pl.BlockSpec((None, d, ff), lambda bb, i: (layer, 0, 1), pipeline_mode=single),
                pl.BlockSpec((None, ff, d), lambda bb, i: (layer, 0, 0), pipeline_mode=single),
                pl.BlockSpec((1, d), lambda bb, i: (0, 0))]
    args = [x, cond, cond, cond, norms, w13, w13, w2, final_g]
    if mixer_ys:
        in_specs.append(_cond_spec(lambda bb: cond_base(bb) + mod0 - 1, d))
        in_specs += [pl.BlockSpec((1, tm, y.shape[2]), lambda bb, i: (bb, i, 0)) for y in mixer_ys]
        in_specs.append(pl.BlockSpec((None,) + w_out.shape[1:], lambda bb, i: (layer, 0, 0),
                                     pipeline_mode=single))
        args += [cond, *mixer_ys, w_out]
    return pl.pallas_call(
        functools.partial(_ffn_body, apply_final=apply_final, mixer_parts=len(mixer_ys)),
        grid=(b, s // tm),
        in_specs=in_specs,
        out_specs=pl.BlockSpec((1, tm, d), lambda bb, i: (bb, i, 0)),
        out_shape=jax.ShapeDtypeStruct((b, s, d), F32),
        compiler_params=_cparams(2),
        name="ffn",
    )(*args)


N_PB = 3 * GROUP_WIDTH
N_PM = 3 * GROUP_WIDTH
N_PR = 5 * GROUP_WIDTH
N_PS = 3 * GROUP_WIDTH
N_PROJ = N_PB + N_PM + N_PR + N_PS
N_PT = 2 * GROUP_WIDTH


def _inproj_body(x_ref, sh_ref, sc_ref, ng_ref, w_ref, wt_ref,
                 pb_ref, pm_ref, pr_ref, ps_ref, pt_ref, gr_ref):
    sub = x_ref.shape[1] // INPROJ_SPLIT

    def row_group(r):
        rs = slice(r * sub, (r + 1) * sub)
        hb = _rms_mod(x_ref[0, rs, :], ng_ref[...], sh_ref[0], sc_ref[0]).astype(BF16)
        yield
        p = _dot(hb, w_ref[...])
        pt = _dot_nt(wt_ref[...], hb)
        yield
        o = 0
        pb_ref[0, rs, :] = p[:, o:o + N_PB].astype(BF16)
        o += N_PB
        pm_ref[0, rs, :] = p[:, o:o + N_PM]
        o += N_PM
        pr_ref[0, rs, :] = p[:, o:o + N_PR]
        o += N_PR
        ps_ref[0, rs, :] = p[:, o:o + N_PS]
        pt_ref[0, :, rs] = pt[0:N_PT, :].astype(BF16)
        gr_ref[0, :, rs] = pt[N_PT:N_PT + GATE_ROWS, :]

    _lockstep(row_group(r) for r in range(INPROJ_SPLIT))


def _inproj(x, cond, cond_base, norms, norm_row, w_all, w_t, layer):
    b, s, d = x.shape
    tm = INPROJ_ROWS
    single = pl.Buffered(1)

    def tok(width):
        return pl.BlockSpec((1, tm, width), lambda bb, i: (bb, i, 0))

    return pl.pallas_call(
        _inproj_body,
        grid=(b, s // tm),
        in_specs=[tok(d),
                  _cond_spec(lambda bb: cond_base(bb) + 3, d),
                  _cond_spec(lambda bb: cond_base(bb) + 4, d),
                  _layer_spec(norms, norm_row),
                  pl.BlockSpec((None, d, N_PROJ), lambda bb, i: (layer, 0, 0), pipeline_mode=single),
                  pl.BlockSpec((None, N_PT + GATE_ROWS, d), lambda bb, i: (layer, 0, 0))],
        out_specs=[tok(N_PB), tok(N_PM), tok(N_PR), tok(N_PS),
                   pl.BlockSpec((1, N_PT, tm), lambda bb, i: (bb, 0, i)),
                   pl.BlockSpec((1, GATE_ROWS, tm), lambda bb, i: (bb, 0, i))],
        out_shape=[jax.ShapeDtypeStruct((b, s, N_PB), BF16),
                   jax.ShapeDtypeStruct((b, s, N_PM), F32),
                   jax.ShapeDtypeStruct((b, s, N_PR), F32),
                   jax.ShapeDtypeStruct((b, s, N_PS), F32),
                   jax.ShapeDtypeStruct((b, N_PT, s), BF16),
                   jax.ShapeDtypeStruct((b, GATE_ROWS, s), F32)],
        compiler_params=_cparams(2),
        name="inproj",
    )(x, cond, cond, norms, w_all, w_t)


def _fox_body(k_ref, qt_ref, vt_ref, gr_ref, bcol_ref, y_ref, cum_c, cum_r, acc_s):
    nb, s_len = k_ref.shape[0], k_ref.shape[1]
    t = ATT_BLOCK
    w = GROUP_WIDTH
    nblk = s_len // t
    _, _, tri_u = _tri(t)
    key_le_query = (lax.broadcasted_iota(jnp.int32, (t, t), 0)
                    <= lax.broadcasted_iota(jnp.int32, (t, t), 1))
    row_head = lax.broadcasted_iota(jnp.int32, (w, t), 0) // HEAD_DIM
    rmask = [jnp.where(row_head == h, 1.0, 0.0).astype(BF16) for h in range(N_HEADS)]
    units = [(bi, h) for bi in range(nb) for h in range(N_HEADS)]
    acc_rows = acc_s.shape[2]
    ones_rows = jnp.ones((acc_rows - HEAD_DIM, t), BF16)

    def cum_step(c, carry):
        r0 = pl.multiple_of(c * t, t)
        out = []
        for bi in range(nb):
            lf_r = _log_sigmoid(gr_ref[bi, :, pl.ds(r0, t)] + bcol_ref[:, 0:1])
            cr = _cumsum_lanes(lf_r, tri_u) + carry[bi]
            cum_r[bi, :, pl.ds(r0, t)] = cr
            cum_c[bi, pl.ds(r0, t), :] = cr.T
            out.append(cr[:, t - 1:t])
        return tuple(out)

    lax.fori_loop(0, nblk, cum_step, tuple(jnp.zeros((GATE_ROWS, 1), F32) for _ in range(nb)))

    def q_step(i, _):
        q0 = pl.multiple_of(i * t, t)
        qts, cts = [], []
        for bi, h in units:
            qts.append(qt_ref[bi, :, pl.ds(q0, t)] * rmask[h])
            cts.append(cum_r[bi, G_FOX + h:G_FOX + h + 1, pl.ds(q0, t)])
        acc_s[...] = jnp.zeros((nb, N_HEADS, acc_rows, t), F32)

        def kv_block(j, maxes, diagonal):
            k0 = pl.multiple_of(j * t, t)
            kbs = [k_ref[bi, pl.ds(k0, t), :] for bi in range(nb)]
            scores = [_dot(kbs[bi], qts[u]) for u, (bi, h) in enumerate(units)]
            new_maxes, probs, alphas = [], [], []
            for u, (bi, h) in enumerate(units):
                cs = cum_c[bi, pl.ds(k0, t), G_FOX + h:G_FOX + h + 1]
                sc = scores[u] + cts[u] - cs
                if diagonal:
                    sc = jnp.where(key_le_query, sc, NEG_BIG)
                m_new = jnp.maximum(maxes[u], jnp.max(sc, axis=0, keepdims=True))
                alphas.append(jnp.exp(maxes[u] - m_new))
                probs.append(jnp.exp(sc - m_new).astype(BF16))
                new_maxes.append(m_new)
            for u, (bi, h) in enumerate(units):
                v_aug = jnp.concatenate(
                    [vt_ref[bi, h * HEAD_DIM:(h + 1) * HEAD_DIM, pl.ds(k0, t)], ones_rows], axis=0)
                acc_s[bi, h] = alphas[u] * acc_s[bi, h] + _dot(v_aug, probs[u])
            return tuple(new_maxes)

        init = tuple(jnp.full((1, t), NEG_BIG, F32) for _ in units)
        maxes = lax.fori_loop(0, i, lambda j, mx: kv_block(j, mx, False), init)
        kv_block(i, maxes, True)
        for bi in range(nb):
            out_t = jnp.concatenate(
                [acc_s[bi, h, 0:HEAD_DIM, :] / acc_s[bi, h, HEAD_DIM:HEAD_DIM + 1, :]
                 for h in range(N_HEADS)], axis=0)
            y_ref[bi, pl.ds(q0, t), :] = out_t.T.astype(BF16)
        return 0

    lax.fori_loop(0, nblk, q_step, 0)


def _fox(pb, pt, gr, bias_cols, layer):
    b, s, _ = pb.shape
    w = GROUP_WIDTH
    t = ATT_BLOCK
    nb = FOX_BATCH
    return pl.pallas_call(
        _fox_body,
        grid=(b // nb,),
        in_specs=[pl.BlockSpec((nb, s, w), lambda bb: (bb, 0, 0)),
                  pl.BlockSpec((nb, w, s), lambda bb: (bb, 0, 0)),
                  pl.BlockSpec((nb, w, s), lambda bb: (bb, 1, 0)),
                  pl.BlockSpec((nb, GATE_ROWS, s), lambda bb: (bb, 0, 0)),
                  _layer_spec(bias_cols, layer)],
        out_specs=pl.BlockSpec((nb, s, w), lambda bb: (bb, 0, 0)),
        out_shape=jax.ShapeDtypeStruct((b, s, w), BF16),
        scratch_shapes=[pltpu.VMEM((nb, s, GATE_ROWS), F32), pltpu.VMEM((nb, GATE_ROWS, s), F32),
                        pltpu.VMEM((nb, N_HEADS, HEAD_DIM + FOX_SUM_ROWS, t), F32)],
        compiler_params=_cparams(1),
        name="fox",
    )(pb, pt, pt, gr, bias_cols)


def _lockstep(gens):
    gens = list(gens)
    while gens:
        alive = []
        for g in gens:
            try:
                next(g)
                alive.append(g)
            except StopIteration:
                pass
        gens = alive


def _ret_body(p_ref, v_ref, lg_ref, inv_ref, sgn_ref, ng_ref, y_ref, cos_s, sin_s, state_s):
    nb, ts = p_ref.shape[0], p_ref.shape[1]
    n = CHUNK
    w = GROUP_WIDTH
    si = pl.program_id(1)
    base = pl.multiple_of(si * ts, ts)
    causal, _, _ = _tri(n)
    head_lane = _head_of_lane()
    hmask = _head_masks(head_lane)
    gmat = _group_mean_matrix()
    bd = _block_diag_mask(w)

    @pl.when(pl.program_id(0) == 0)
    def _():
        def tab(c, _):
            r0 = pl.multiple_of(base + c * n, n)
            pos = (lax.broadcasted_iota(jnp.int32, (n, 1), 0) + r0).astype(F32)
            ang = pos * inv_ref[...]
            cos_s[pl.ds(r0, n), :] = jnp.cos(ang)
            sin_s[pl.ds(r0, n), :] = jnp.sin(ang) * sgn_ref[...]
            return 0
        lax.fori_loop(0, ts // n, tab, 0)

    @pl.when(si == 0)
    def _():
        state_s[...] = jnp.zeros((nb, w, w), F32)

    tcol = (lax.broadcasted_iota(jnp.int32, (n, 1), 0) + 1).astype(F32)
    trow = (lax.broadcasted_iota(jnp.int32, (1, n), 1) + 1).astype(F32)
    lgs = [lg_ref[0:1, h:h + 1] for h in range(N_HEADS)]
    decays = [jnp.where(causal, jnp.exp(tcol * lgs[h] - trow * lgs[h]), 0.0) for h in range(N_HEADS)]
    q_scale = _expand([jnp.exp(tcol * lgs[h]) for h in range(N_HEADS)], head_lane)
    w_end = _expand([jnp.exp(n * lgs[h] - tcol * lgs[h]) for h in range(N_HEADS)], head_lane)
    a_chunk = _expand([jnp.exp(n * lgs[h]) for h in range(N_HEADS)], head_lane)

    def per_batch(bi, c):
        r0 = pl.multiple_of(c * n, n)
        rows = pl.ds(r0, n)
        trows = pl.ds(pl.multiple_of(base + r0, n), n)
        cos = cos_s[trows, :]
        sin = sin_s[trows, :]
        q = p_ref[bi, rows, 0:w] * cos + p_ref[bi, rows, w:2 * w] * sin
        k = p_ref[bi, rows, 2 * w:3 * w] * cos + p_ref[bi, rows, 3 * w:4 * w] * sin
        v = v_ref[bi, rows, :]
        qb = q.astype(BF16)
        kb = k.astype(BF16)
        qs = _dot(qb, state_s[bi].astype(BF16))
        qk = _dot_nt(jnp.concatenate([qb * hmask[h] for h in range(N_HEADS)], axis=0), kb)
        yield
        probs = [(qk[h * n:(h + 1) * n, :] * decays[h]).astype(BF16) for h in range(N_HEADS)]
        pv = _dot(jnp.concatenate(probs, axis=1),
                  jnp.concatenate([v * hmask[h] for h in range(N_HEADS)], axis=0))
        kv = _dot_tn((k * w_end).astype(BF16), v)
        yield
        y = qs * q_scale + pv
        state_s[bi] = state_s[bi] * a_chunk + jnp.where(bd, kv, 0.0)
        msq = _head_mean_sq(y, gmat)
        yield
        yn = y * lax.rsqrt(msq + RMS_EPS) * ng_ref[...]
        y_ref[bi, rows, :] = (_silu(p_ref[bi, rows, 4 * w:5 * w]) * yn).astype(BF16)

    def chunk(c, _):
        _lockstep(per_batch(bi, c) for bi in range(nb))
        return 0

    lax.fori_loop(0, ts // n, chunk, 0)


def _mixer_specs(widths_and_blocks):
    return [pl.BlockSpec((MIX_BATCH, MIX_ROWS, width), lambda bb, ss, blk=blk: (bb, ss, blk))
            for width, blk in widths_and_blocks]


def _mixer_const(shape):
    return pl.BlockSpec(shape, lambda bb, ss: (0,) * len(shape))


def _ret(pr, pb, lg_row, inv_row, sgn_row, norm_gs, layer):
    b, s, _ = pr.shape
    w = GROUP_WIDTH
    return pl.pallas_call(
        _ret_body,
        grid=(b // MIX_BATCH, s // MIX_ROWS),
        in_specs=_mixer_specs([(N_PR, 0), (w, 2)]) + [
            _mixer_const((1, LANES)), _mixer_const((1, w)), _mixer_const((1, w)),
            _layer_spec(norm_gs, layer)],
        out_specs=_mixer_specs([(w, 0)])[0],
        out_shape=jax.ShapeDtypeStruct((b, s, w), BF16),
        scratch_shapes=[pltpu.VMEM((s, w), F32), pltpu.VMEM((s, w), F32),
                        pltpu.VMEM((MIX_BATCH, w, w), F32)],
        compiler_params=_cparams(2),
        name="retention",
    )(pr, pb, lg_row, inv_row, sgn_row, norm_gs)


def _ssd_body(p_ref, gr_ref, cw_ref, cb_ref, bcol_ref, acol_ref, dskip_ref, ng_ref, y_ref,
              state_s, tail_s, x_s, q_s, k_s, v_s, vw_s, y_s, crow_s, ccol_s, ctmat_s, qscale_s,
              dtl_s, wend_s):
    nb, ts = p_ref.shape[0], p_ref.shape[1]
    n = CHUNK
    w = GROUP_WIDTH
    causal, _, _ = _tri(n)
    head_lane = _head_of_lane()
    hmask = _head_masks(head_lane)
    gw = SSD_GROUPS * HEAD_DIM
    hpg = N_HEADS // SSD_GROUPS
    group_lane = lax.broadcasted_iota(jnp.int32, (1, gw), 1) // HEAD_DIM
    gmask = [jnp.where(group_lane == g, 1.0, 0.0).astype(BF16) for g in range(SSD_GROUPS)]
    state_mask = (lax.broadcasted_iota(jnp.int32, (gw, w), 0) // HEAD_DIM
                  == lax.broadcasted_iota(jnp.int32, (gw, w), 1) // (HEAD_DIM * hpg))

    @pl.when(pl.program_id(1) == 0)
    def _():
        state_s[...] = jnp.zeros((nb, gw, w), F32)
        tail_s[...] = jnp.zeros((nb, 8, 2 * w), F32)

    batch = range(nb)
    n_chunks = ts // n
    half = G_DT - 8
    sel_row = lax.broadcasted_iota(jnp.int32, (16, 1), 0) - half
    sel_chunk = jnp.where(
        sel_row == lax.broadcasted_iota(jnp.int32, (1, N_HEADS * n), 1) // n, 1.0, 0.0).astype(BF16)
    sel_head = jnp.where(sel_row == head_lane, 1.0, 0.0).astype(BF16)
    r_i = lax.broadcasted_iota(jnp.int32, (n, 2 * n), 0)
    c_i = lax.broadcasted_iota(jnp.int32, (n, 2 * n), 1)
    scan_mat = jnp.where(((c_i < n) & (r_i <= c_i)) | ((c_i >= n) & (r_i > c_i - n)), 1.0, 0.0
                         ).astype(BF16)
    dt16s = [_softplus(gr_ref[bi] + bcol_ref[:, 0:1]) for bi in batch]
    la_stack = jnp.concatenate([(dt16s[bi] * acol_ref[:, 0:1])[:, c * n:(c + 1) * n]
                                for bi in batch for c in range(n_chunks)], axis=0)
    scans = _cumsum_lanes(la_stack, scan_mat)

    def unstack(bi, lo):
        return jnp.concatenate(
            [scans[(bi * n_chunks + c) * GATE_ROWS + 8:(bi * n_chunks + c + 1) * GATE_ROWS, lo:lo + n]
             for c in range(n_chunks)], axis=1)

    cum8s = [unstack(bi, 0) for bi in batch]
    rev8s = [unstack(bi, n) for bi in batch]
    dt8s = [dt16[8:16, :] for dt16 in dt16s]
    ct_mats = [_rows_to_lanes(c8, sel_chunk) for c8 in cum8s]
    q_scales = [jnp.exp(_rows_to_lanes(c8, sel_head)) for c8 in cum8s]
    dt_lanes = [_rows_to_lanes(d8, sel_head) for d8 in dt8s]
    w_ends = [jnp.exp(_rows_to_lanes(r8, sel_head)) for r8 in rev8s]
    for bi in batch:
        crow_s[bi] = cum8s[bi]
        ccol_s[bi] = cum8s[bi].T
        for h in range(N_HEADS):
            ctmat_s[bi, h] = ct_mats[bi][:, h * n:(h + 1) * n]
        qscale_s[bi] = q_scales[bi]
        dtl_s[bi] = dt_lanes[bi]
        wend_s[bi] = w_ends[bi]

    def conv_step(c, _):
        rows = pl.ds(pl.multiple_of(c * n, n), n)
        for bi in batch:
            pre = p_ref[bi, rows, w:3 * w]
            xbc = _causal_conv_silu(pre, tail_s[bi], cw_ref, cb_ref)
            tail_s[bi] = pre[n - 8:n, :]
            x = xbc[:, 0:w]
            x_s[bi, rows, :] = x
            k_s[bi, rows, :] = xbc[:, w:w + gw].astype(BF16)
            q_s[bi, rows, :] = xbc[:, w + gw:w + 2 * gw].astype(BF16)
            v = x * dtl_s[bi, rows, :]
            v_s[bi, rows, :] = v.astype(BF16)
            vw_s[bi, rows, :] = (v * wend_s[bi, rows, :]).astype(BF16)
        return 0

    lax.fori_loop(0, n_chunks, conv_step, 0)

    def per_batch(bi, c):
        rows = pl.ds(pl.multiple_of(c * n, n), n)
        qb = q_s[bi, rows, :]
        kb = k_s[bi, rows, :]
        vb = v_s[bi, rows, :]
        qs = _dot(qb, state_s[bi].astype(BF16))
        qk = _dot_nt(jnp.concatenate([qb * gmask[g] for g in range(SSD_GROUPS)], axis=0), kb)
        kv = _dot_tn(kb, vw_s[bi, rows, :])
        yield
        c_rows = crow_s[bi, :, rows]
        probs = [(qk[(h // hpg) * n:(h // hpg + 1) * n, :]
                  * jnp.where(causal, jnp.exp(ctmat_s[bi, h, rows, :]
                                              - c_rows[half + h:half + h + 1, :]), 0.0)
                  ).astype(BF16) for h in range(N_HEADS)]
        pv = _dot(jnp.concatenate(probs, axis=1),
                  jnp.concatenate([vb * hmask[h] for h in range(N_HEADS)], axis=0))
        yield
        y_s[bi, rows, :] = qs * qscale_s[bi, rows, :] + pv
        c_end = ccol_s[bi, pl.ds(pl.multiple_of(c * n, n) + n - 8, 8), :][7:8, :]
        a_chunk = _expand([jnp.exp(c_end[:, half + h:half + h + 1]) for h in range(N_HEADS)],
                          head_lane)
        state_s[bi] = state_s[bi] * a_chunk + jnp.where(state_mask, kv, 0.0)

    def chunk(c, _):
        _lockstep(per_batch(bi, c) for bi in batch)
        return 0

    lax.fori_loop(0, n_chunks, chunk, 0)

    def norm_step(c, _):
        rows = pl.ds(pl.multiple_of(c * n, n), n)
        for bi in batch:
            hs = (y_s[bi, rows, :] + dskip_ref[...] * x_s[bi, rows, :]) * _silu(p_ref[bi, rows, 0:w])
            yn = hs * lax.rsqrt(jnp.mean(hs * hs, axis=-1, keepdims=True) + RMS_EPS) * ng_ref[...]
            y_ref[bi, rows, :] = yn.astype(BF16)
        return 0

    lax.fori_loop(0, n_chunks, norm_step, 0)


def _gate_row_spec():
    return pl.BlockSpec((MIX_BATCH, GATE_ROWS, MIX_ROWS), lambda bb, ss: (bb, 0, ss))


def _ssd(ps, gr, conv_ws, conv_bs, bias_cols, a_cols, d_rows, norm_gs, layer):
    b, s, _ = ps.shape
    w = GROUP_WIDTH
    gw = SSD_GROUPS * HEAD_DIM
    nb = MIX_BATCH
    params = (conv_ws, conv_bs, bias_cols, a_cols, d_rows, norm_gs)
    return pl.pallas_call(
        _ssd_body,
        grid=(b // MIX_BATCH, s // MIX_ROWS),
        in_specs=_mixer_specs([(N_PS, 0)]) + [_gate_row_spec()] + [
            _layer_spec(p, layer) for p in params],
        out_specs=_mixer_specs([(w, 0)])[0],
        out_shape=jax.ShapeDtypeStruct((b, s, w), BF16),
        scratch_shapes=[pltpu.VMEM((nb, gw, w), F32),
                        pltpu.VMEM((nb, 8, 2 * w), F32),
                        pltpu.VMEM((nb, MIX_ROWS, w), F32),
                        pltpu.VMEM((nb, MIX_ROWS, gw), BF16),
                        pltpu.VMEM((nb, MIX_ROWS, gw), BF16),
                        pltpu.VMEM((nb, MIX_ROWS, w), BF16),
                        pltpu.VMEM((nb, MIX_ROWS, w), BF16),
                        pltpu.VMEM((nb, MIX_ROWS, w), F32),
                        pltpu.VMEM((nb, 8, MIX_ROWS), F32),
                        pltpu.VMEM((nb, MIX_ROWS, 8), F32),
                        pltpu.VMEM((nb, N_HEADS, MIX_ROWS, CHUNK), F32),
                        pltpu.VMEM((nb, MIX_ROWS, w), F32),
                        pltpu.VMEM((nb, MIX_ROWS, w), F32),
                        pltpu.VMEM((nb, MIX_ROWS, w), F32)],
        compiler_params=_cparams(2),
        name="ssd",
    )(ps, gr, *params)


def _rows_to_lanes(x8, sel):
    parts = _split3(jnp.concatenate([x8, jnp.zeros_like(x8)], axis=0))
    return _dot_tn(jnp.concatenate(parts, axis=0), jnp.concatenate([sel, sel, sel], axis=0))


def _chunk_cummax(xs, n):
    lane_in_chunk = lax.broadcasted_iota(jnp.int32, (1, xs[0].shape[1]), 1) % n
    shift = 1
    while shift < n:
        keep = lane_in_chunk >= shift
        xs = [jnp.maximum(x, jnp.where(keep, pltpu.roll(x, shift, 1), NEG_BIG)) for x in xs]
        shift *= 2
    return xs


def _mlstm_body(p_ref, v_ref, gr_ref, cw_ref, cb_ref, bcol_ref, ng_ref, y_ref,
                state_s, m_s, tail_s, q_s, k_s, h_s, rrow_s, colb_s, colc_s, cmat_s, cexp_s,
                rexp_s):
    nb, ts = p_ref.shape[0], p_ref.shape[1]
    n = CHUNK
    w = GROUP_WIDTH
    wa = w + LANES
    causal, _, _ = _tri(n)
    head_lane = _head_of_lane()
    hmask = _head_masks(head_lane)
    head_lane_aug = jnp.concatenate(
        [head_lane, lax.broadcasted_iota(jnp.int32, (1, LANES), 1)], axis=1)
    gmat = _group_mean_matrix()
    bd_aug = _block_diag_mask(wa)
    lane128 = lax.broadcasted_iota(jnp.int32, (1, LANES), 1)
    ones_col = [jnp.where(lane128 == h, 1.0, 0.0).astype(BF16) for h in range(N_HEADS)]

    @pl.when(pl.program_id(1) == 0)
    def _():
        state_s[...] = jnp.zeros((nb, w, wa), F32)
        m_s[...] = jnp.zeros((nb, 1, 8), F32)
        tail_s[...] = jnp.zeros((nb, 8, 2 * w), F32)

    _, _, tri_u = _tri(n)
    sel_row = lax.broadcasted_iota(jnp.int32, (16, 1), 0)
    sel_chunk = jnp.where(
        sel_row == lax.broadcasted_iota(jnp.int32, (1, N_HEADS * n), 1) // n, 1.0, 0.0).astype(BF16)
    sel_head = jnp.where(sel_row == head_lane, 1.0, 0.0).astype(BF16)
    batch = range(nb)
    n_chunks = ts // n
    g_blks = [gr_ref[bi] + bcol_ref[:, 0:1] for bi in batch]
    lf_blks = [_log_sigmoid(g) for g in g_blks]
    b_stack = _cumsum_lanes(
        jnp.concatenate([lf[:, c * n:(c + 1) * n] for lf in lf_blks for c in range(n_chunks)], axis=0),
        tri_u)
    b16s = [jnp.concatenate([b_stack[(bi * n_chunks + c) * GATE_ROWS:(bi * n_chunks + c + 1) * GATE_ROWS]
                             for c in range(n_chunks)], axis=1) for bi in batch]
    r16s = [pltpu.roll(g_blks[bi], G_MF - G_MI, 0) - b16s[bi] for bi in batch]
    b8s = [b16[G_MF:G_MF + 8, :] for b16 in b16s]
    r8s = [r16[G_MF:G_MF + 8, :] for r16 in r16s]
    c8s = _chunk_cummax(r8s, n)
    c_mats = [_rows_to_lanes(c8, sel_chunk) for c8 in c8s]
    c_exps = [_rows_to_lanes(c8, sel_head) for c8 in c8s]
    r_exps = [_rows_to_lanes(r8, sel_head) for r8 in r8s]
    for bi in batch:
        rrow_s[bi] = r8s[bi]
        colb_s[bi] = b8s[bi].T
        colc_s[bi] = c8s[bi].T
        for h in range(N_HEADS):
            cmat_s[bi, h] = c_mats[bi][:, h * n:(h + 1) * n]
        cexp_s[bi] = c_exps[bi]
        rexp_s[bi] = r_exps[bi]
    def conv_step(c, _):
        rows = pl.ds(pl.multiple_of(c * n, n), n)
        for bi in batch:
            pre = p_ref[bi, rows, 0:2 * w]
            qk_act = _causal_conv_silu(pre, tail_s[bi], cw_ref, cb_ref)
            tail_s[bi] = pre[n - 8:n, :]
            q_s[bi, rows, :] = (qk_act[:, 0:w] * (HEAD_DIM ** -0.5)).astype(BF16)
            k_s[bi, rows, :] = qk_act[:, w:2 * w]
        return 0

    lax.fori_loop(0, n_chunks, conv_step, 0)

    def per_batch(bi, c):
        r0 = pl.multiple_of(c * n, n)
        rows = pl.ds(r0, n)
        qb = q_s[bi, rows, :]
        k = k_s[bi, rows, :]
        kb = k.astype(BF16)
        vb = v_ref[bi, rows, :]

        qs = _dot(qb, state_s[bi].astype(BF16))
        qk = _dot_nt(jnp.concatenate([qb * hmask[h] for h in range(N_HEADS)], axis=0), kb)
        yield

        heads = range(N_HEADS)
        m8 = m_s[bi]
        m_ins = [m8[:, h:h + 1] for h in heads]
        r_rows = rrow_s[bi, :, rows]
        col_b = colb_s[bi, rows, :]
        col_c = colc_s[bi, rows, :]
        c_end8 = col_c[n - 1:n, :]
        z_end8 = jnp.maximum(m8, c_end8)
        probs = [(jnp.where(causal,
                            jnp.exp(r_rows[h:h + 1, :] - jnp.maximum(m_ins[h], cmat_s[bi, h, rows, :])),
                            0.0) * qk[h * n:(h + 1) * n, :]).astype(BF16) for h in heads]
        v_rows = jnp.concatenate(
            [jnp.concatenate([vb * hmask[h], jnp.broadcast_to(ones_col[h], (n, LANES))], axis=1)
             for h in heads], axis=0)
        pv = _dot(jnp.concatenate(probs, axis=1), v_rows)
        c_ends = [c_end8[:, h:h + 1] for h in heads]
        kw = (k * jnp.exp(rexp_s[bi, rows, :] - _expand(c_ends, head_lane))).astype(BF16)
        v_aug = jnp.concatenate([vb, jnp.ones((n, LANES), BF16)], axis=1)
        kv = _dot_tn(kw, v_aug)
        yield
        m_lanes = _expand(m_ins, head_lane)
        a_inter = jnp.exp(m_lanes - jnp.maximum(m_lanes, cexp_s[bi, rows, :]))
        num = pv[:, 0:w] + a_inter * qs[:, 0:w]
        z8 = jnp.maximum(m8, col_c)
        den8 = pv[:, w:w + 8] + jnp.exp(m8 - z8) * qs[:, w:w + 8]
        dens8 = jnp.maximum(jnp.abs(den8), jnp.exp(-(col_b + z8)))
        h_s[bi, rows, :] = num / _expand([dens8[:, h:h + 1] for h in heads], head_lane)
        a_old8 = jnp.exp(m8 - z_end8)
        a_loc8 = jnp.exp(c_end8 - z_end8)
        state_s[bi] = (state_s[bi] * _expand([a_old8[:, h:h + 1] for h in heads], head_lane_aug)
                       + jnp.where(bd_aug, kv, 0.0)
                       * _expand([a_loc8[:, h:h + 1] for h in heads], head_lane_aug))
        m_s[bi] = col_b[n - 1:n, :] + z_end8

    def chunk(c, _):
        _lockstep(per_batch(bi, c) for bi in range(nb))
        return 0

    lax.fori_loop(0, ts // n, chunk, 0)

    def norm_step(c, _):
        rows = pl.ds(pl.multiple_of(c * n, n), n)
        hvals = [h_s[bi, rows, :] for bi in batch]
        msqs = [_head_mean_sq(hv, gmat) for hv in hvals]
        for bi in batch:
            yn = hvals[bi] * lax.rsqrt(msqs[bi] + RMS_EPS) * ng_ref[...]
            y_ref[bi, rows, :] = (_sigmoid(p_ref[bi, rows, 2 * w:3 * w]) * yn).astype(BF16)
        return 0

    lax.fori_loop(0, n_chunks, norm_step, 0)


def _mlstm(pm, pb, gr, conv_ws, conv_bs, bias_cols, norm_gs, layer):
    b, s, _ = pm.shape
    w = GROUP_WIDTH
    params = (conv_ws, conv_bs, bias_cols, norm_gs)
    return pl.pallas_call(
        _mlstm_body,
        grid=(b // MIX_BATCH, s // MIX_ROWS),
        in_specs=_mixer_specs([(N_PM, 0), (w, 1)]) + [_gate_row_spec()] + [
            _layer_spec(p, layer) for p in params],
        out_specs=_mixer_specs([(w, 0)])[0],
        out_shape=jax.ShapeDtypeStruct((b, s, w), BF16),
        scratch_shapes=[pltpu.VMEM((MIX_BATCH, w, w + LANES), F32),
                        pltpu.VMEM((MIX_BATCH, 1, 8), F32),
                        pltpu.VMEM((MIX_BATCH, 8, 2 * w), F32),
                        pltpu.VMEM((MIX_BATCH, MIX_ROWS, w), BF16),
                        pltpu.VMEM((MIX_BATCH, MIX_ROWS, w), F32),
                        pltpu.VMEM((MIX_BATCH, MIX_ROWS, w), F32),
                        pltpu.VMEM((MIX_BATCH, 8, MIX_ROWS), F32),
                        pltpu.VMEM((MIX_BATCH, MIX_ROWS, 8), F32),
                        pltpu.VMEM((MIX_BATCH, MIX_ROWS, 8), F32),
                        pltpu.VMEM((MIX_BATCH, N_HEADS, MIX_ROWS, CHUNK), F32),
                        pltpu.VMEM((MIX_BATCH, MIX_ROWS, w), F32),
                        pltpu.VMEM((MIX_BATCH, MIX_ROWS, w), F32)],
        compiler_params=_cparams(2),
        name="mlstm",
    )(pm, pb, gr, *params)


def _half_swap(t):
    lead = t.shape[:-1]
    t = t.reshape(lead + (N_HEADS, 2, HEAD_DIM // 2))
    return t[..., ::-1, :].reshape(lead + (GROUP_WIDTH,))


def _inproj_weights(w_in):
    w, h, hd = GROUP_WIDTH, N_HEADS, HEAD_DIM
    w_f32 = w_in
    w_in = w_in.astype(BF16)
    qk_scale = hd ** -0.5
    fox0 = 0
    ml0 = fox0 + 3 * w + h
    ret0 = ml0 + 4 * w + 2 * h
    ssd0 = ret0 + 4 * w
    bc = 2 * SSD_GROUPS * hd

    def col(start, width, src=None):
        return (w_in if src is None else src)[:, :, start:start + width]

    rq, rk = col(ret0, w), col(ret0 + w, w) * qk_scale
    gates = jnp.concatenate([col(fox0 + 3 * w, h, w_f32), col(ml0 + 3 * w, 2 * h, w_f32),
                             col(ssd0 + 2 * w + bc, h, w_f32)], axis=2)
    w_t = jnp.concatenate([col(fox0, w, w_f32) * qk_scale, col(fox0 + 2 * w, w, w_f32), gates],
                          axis=2)
    w_all = jnp.concatenate([
        col(fox0 + w, w), col(ml0 + 2 * w, w), col(ret0 + 2 * w, w),
        col(ml0, 2 * w), col(ml0 + 3 * w + 2 * h, w),
        rq, _half_swap(rq), rk, _half_swap(rk), col(ret0 + 3 * w, w),
        col(ssd0, w), col(ssd0 + w, w + bc)], axis=2)
    return w_all, jnp.swapaxes(w_t, 1, 2).astype(BF16)


def _lane_pad(v, width=LANES):
    return jnp.zeros((width,), F32).at[:v.shape[0]].set(v)


def kernel(x, c, ada_w, ada_b, norm_g, ffn1_w13, ffn1_w2, ffn2_w13, ffn2_w2, w_in, w_out, fox_fb,
           mlstm_conv_w, mlstm_conv_b, mlstm_ib, mlstm_fb, mlstm_norm_g, ret_norm_g, ssd_conv_w,
           ssd_conv_b, ssd_dt_bias, ssd_A_log, ssd_D, ssd_norm_g, final_g):
    n_layers = ada_w.shape[0]
    b, s, d = x.shape
    hd = HEAD_DIM

    cond = _adaln(c, ada_w, ada_b).reshape(n_layers * b * N_MOD, 1, d)

    inv = 1.0 / (ROPE_BASE ** (jnp.arange(0, hd, 2, dtype=F32) / hd))
    inv_row = jnp.tile(inv, 2 * N_HEADS).reshape(1, GROUP_WIDTH)
    sgn_row = jnp.tile(jnp.concatenate([-jnp.ones(hd // 2, F32), jnp.ones(hd // 2, F32)]),
                       N_HEADS).reshape(1, GROUP_WIDTH)
    log_gamma = jnp.log(1.0 - 2.0 ** (-RET_DECAY_OFFSET - jnp.arange(N_HEADS, dtype=F32)))
    lg_row = _lane_pad(log_gamma).reshape(1, LANES)

    w13_a, w2_a = ffn1_w13.astype(BF16), ffn1_w2.astype(BF16)
    w13_b, w2_b = ffn2_w13.astype(BF16), ffn2_w2.astype(BF16)
    w_out_b = w_out.astype(BF16)
    w_all, w_t = _inproj_weights(w_in)

    norms = norm_g.reshape(n_layers * 3, 1, d)
    final_row = final_g.reshape(1, d)
    gate_bias = jnp.concatenate([fox_fb, mlstm_ib, mlstm_fb, ssd_dt_bias], axis=1)
    bias_cols = jnp.broadcast_to(gate_bias[:, :, None], (n_layers, GATE_ROWS, LANES))
    a_rows = jnp.zeros((n_layers, GATE_ROWS), F32).at[:, G_DT:G_DT + N_HEADS].set(
        -jnp.exp(ssd_A_log.astype(F32)))
    a_cols = jnp.broadcast_to(a_rows[:, :, None], (n_layers, GATE_ROWS, LANES))
    d_rows = jnp.repeat(ssd_D, hd, axis=1).reshape(n_layers, 1, GROUP_WIDTH)
    mlstm_cb = mlstm_conv_b.reshape(n_layers, 1, -1)
    ssd_cb = ssd_conv_b.reshape(n_layers, 1, -1)
    mlstm_ng = mlstm_norm_g.reshape(n_layers, 1, GROUP_WIDTH)
    ret_ng = ret_norm_g.reshape(n_layers, 1, GROUP_WIDTH)
    ssd_ng = ssd_norm_g.reshape(n_layers, 1, GROUP_WIDTH)

    for l in range(n_layers):
        def cond_base(bb, l=l):
            return (l * b + bb) * N_MOD

        x = _ffn(x, cond, cond_base, 0, norms, 3 * l, w13_a, w2_a, l, final_row, False)
        pb, pm, pr, ps, pt, gr = _inproj(x, cond, cond_base, norms, 3 * l + 1, w_all, w_t, l)

        y_fox = _fox(pb, pt, gr, bias_cols, l)
        y_mlstm = _mlstm(pm, pb, gr, mlstm_conv_w, mlstm_cb, bias_cols, mlstm_ng, l)
        y_ret = _ret(pr, pb, lg_row, inv_row, sgn_row, ret_ng, l)
        y_ssd = _ssd(ps, gr, ssd_conv_w, ssd_cb, bias_cols, a_cols, d_rows, ssd_ng, l)

        x = _ffn(x, cond, cond_base, 6, norms, 3 * l + 2, w13_b, w2_b, l, final_row,
                 l == n_layers - 1, mixer_ys=(y_fox, y_mlstm, y_ret, y_ssd), w_out=w_out_b)
    return x
```

```python
import functools

import jax
import jax.numpy as jnp
from jax import lax
from jax.experimental import pallas as pl
from jax.experimental.pallas import tpu as pltpu

F32 = jnp.float32
BF16 = jnp.bfloat16

N_HEADS = 4
HEAD_DIM = 64
GROUP_WIDTH = N_HEADS * HEAD_DIM
CONV_K = 4
SSD_GROUPS = 2
N_MOD = 9
ROPE_BASE = 10000.0
RET_DECAY_OFFSET = 5.0
RMS_EPS = 1e-6
NEG_BIG = -1e30

VMEM_LIMIT_BYTES = 56 * 1024 * 1024
LANES = 128
GATE_ROWS = 16

G_FOX, G_MI, G_MF, G_DT = 0, 4, 8, 12

FFN_ROWS = 1024
FFN_SPLIT = 4
INPROJ_ROWS = 512
INPROJ_SPLIT = 2
ATT_BLOCK = 256
FOX_SUM_ROWS = 16
FOX_BATCH = 4
CHUNK = 128
MIX_BATCH = 4
MIX_ROWS = 512


def _cparams(n_axes):
    return pltpu.CompilerParams(dimension_semantics=("arbitrary",) * n_axes,
                                vmem_limit_bytes=VMEM_LIMIT_BYTES)


def _sigmoid(x):
    return 1.0 / (1.0 + jnp.exp(-x))


def _silu(x):
    return x * _sigmoid(x)


def _softplus(x):
    return jnp.maximum(x, 0.0) + jnp.log1p(jnp.exp(-jnp.abs(x)))


def _log_sigmoid(x):
    return -_softplus(-x)


def _rms_mod(x, g, sh, sc):
    y = x * lax.rsqrt(jnp.mean(x * x, axis=-1, keepdims=True) + RMS_EPS) * g
    return y * (1.0 + sc) + sh


def _dot(a, b):
    return jnp.dot(a, b, preferred_element_type=F32)


def _dot_nt(a, b):
    return lax.dot_general(a, b, (((1,), (1,)), ((), ())), preferred_element_type=F32)


def _dot_tn(a, b):
    return lax.dot_general(a, b, (((0,), (0,)), ((), ())), preferred_element_type=F32)


def _split3(x):
    hi = x.astype(BF16)
    r1 = x - hi.astype(F32)
    mid = r1.astype(BF16)
    lo = (r1 - mid.astype(F32)).astype(BF16)
    return hi, mid, lo


def _cumsum_lanes(x, tri_upper):
    return _dot(jnp.concatenate(_split3(x), axis=1),
                jnp.concatenate([tri_upper, tri_upper, tri_upper], axis=0))


def _tri(n):
    r = lax.broadcasted_iota(jnp.int32, (n, n), 0)
    c = lax.broadcasted_iota(jnp.int32, (n, n), 1)
    return r >= c, jnp.where(r <= c, 1.0, 0.0).astype(BF16)


def _head_of_lane(width=GROUP_WIDTH):
    return lax.broadcasted_iota(jnp.int32, (1, width), 1) // HEAD_DIM


def _head_masks(head_lane):
    return [jnp.where(head_lane == h, 1.0, 0.0).astype(BF16) for h in range(N_HEADS)]


def _expand(vals, head_lane):
    out = vals[N_HEADS - 1]
    for h in range(N_HEADS - 2, -1, -1):
        out = jnp.where(head_lane == h, vals[h], out)
    return out


def _head_mean_sq(y, gmat):
    return _dot(jnp.concatenate(_split3(y * y), axis=1), jnp.concatenate([gmat, gmat, gmat], axis=0))


def _group_mean_matrix():
    r = lax.broadcasted_iota(jnp.int32, (GROUP_WIDTH, GROUP_WIDTH), 0) // HEAD_DIM
    c = lax.broadcasted_iota(jnp.int32, (GROUP_WIDTH, GROUP_WIDTH), 1) // HEAD_DIM
    return jnp.where(r == c, 1.0 / HEAD_DIM, 0.0).astype(BF16)


def _block_diag_mask(cols):
    r = lax.broadcasted_iota(jnp.int32, (GROUP_WIDTH, cols), 0) // HEAD_DIM
    c = lax.broadcasted_iota(jnp.int32, (GROUP_WIDTH, cols), 1)
    same = (c < GROUP_WIDTH) & (r == c // HEAD_DIM)
    if cols > GROUP_WIDTH:
        same = same | ((c >= GROUP_WIDTH) & (r == c - GROUP_WIDTH))
    return same


def _causal_conv_silu(x, prev_tail, w_ref, b_ref):
    n = x.shape[0]
    row8 = lax.broadcasted_iota(jnp.int32, (8, 1), 0)
    x8 = x[0:8, :]
    acc = x * w_ref[CONV_K - 1:CONV_K, :] + b_ref[...]
    acc8 = acc[0:8, :]
    for j in range(1, CONV_K):
        wj = w_ref[CONV_K - 1 - j:CONV_K - j, :]
        acc = acc + pltpu.roll(x, j, 0) * wj
        acc8 = acc8 + jnp.where(row8 < j, pltpu.roll(prev_tail, j, 0), pltpu.roll(x8, j, 0)) * wj
    return _silu(jnp.concatenate([acc8, acc[8:n, :]], axis=0))


def _adaln_body(c_ref, w_ref, b_ref, o_ref):
    c = c_ref[...]
    o_ref[0] = _dot(_silu(c).astype(BF16), w_ref[0].astype(BF16)) + b_ref[0]


def _adaln(c, ada_w, ada_b):
    n_layers, d, n_out = ada_w.shape
    b = c.shape[0]
    tn = 1152
    return pl.pallas_call(
        _adaln_body,
        grid=(n_layers, n_out // tn),
        in_specs=[pl.BlockSpec((b, d), lambda l, j: (0, 0)),
                  pl.BlockSpec((1, d, tn), lambda l, j: (l, 0, j)),
                  pl.BlockSpec((1, 1, tn), lambda l, j: (l, 0, j))],
        out_specs=pl.BlockSpec((1, b, tn), lambda l, j: (l, 0, j)),
        out_shape=jax.ShapeDtypeStruct((n_layers, b, n_out), F32),
        compiler_params=_cparams(2),
        name="adaln",
    )(c, ada_w, ada_b.reshape(n_layers, 1, n_out))


def _layer_spec(stack, row):
    return pl.BlockSpec((None,) + stack.shape[1:], lambda *_: (row,) + (0,) * (stack.ndim - 1))


def _cond_spec(cond_row, d):
    return pl.BlockSpec((1, 1, d), lambda b, i: (cond_row(b), 0, 0))


def _ffn_body(x_ref, sh_ref, sc_ref, gt_ref, ng_ref, w1_ref, w3_ref, w2_ref, fg_ref, *rest,
              apply_final, mixer_parts):
    o_ref = rest[-1]
    tm = x_ref.shape[1]
    sub = tm // FFN_SPLIT

    def row_group(r):
        rs = slice(r * sub, (r + 1) * sub)
        x = x_ref[0, rs, :]
        if mixer_parts:
            mg_ref, wo_ref = rest[0], rest[1 + mixer_parts]
            w = wo_ref.shape[0] // mixer_parts
            proj = _dot(rest[1][0, rs, :], wo_ref[0:w, :])
            for part in range(1, mixer_parts):
                proj += _dot(rest[1 + part][0, rs, :], wo_ref[part * w:(part + 1) * w, :])
            yield
            x = x + mg_ref[0] * proj
        hb = _rms_mod(x, ng_ref[...], sh_ref[0], sc_ref[0]).astype(BF16)
        yield
        a = _dot(hb, w1_ref[...])
        g = _dot(hb, w3_ref[...])
        yield
        act = (_silu(g) * a).astype(BF16)
        yield
        down = _dot(act, w2_ref[...])
        yield
        y = x + (0.5 * gt_ref[0]) * down
        if apply_final:
            y = y * lax.rsqrt(jnp.mean(y * y, axis=-1, keepdims=True) + RMS_EPS) * fg_ref[...]
        o_ref[0, rs, :] = y

    _lockstep(row_group(r) for r in range(FFN_SPLIT))


def _ffn(x, cond, cond_base, mod0, norms, norm_row, w13, w2, layer, final_g, apply_final,
         mixer_ys=(), w_out=None):
    b, s, d = x.shape
    ff = w2.shape[1]
    tm = FFN_ROWS
    single = pl.Buffered(1)
    in_specs = [pl.BlockSpec((1, tm, d), lambda bb, i: (bb, i, 0)),
                _cond_spec(lambda bb: cond_base(bb) + mod0, d),
                _cond_spec(lambda bb: cond_base(bb) + mod0 + 1, d),
                _cond_spec(lambda bb: cond_base(bb) + mod0 + 2, d),
                _layer_spec(norms, norm_row),
                pl.BlockSpec((None, d, ff), lambda bb, i: (layer, 0, 0), pipeline_mode=single),
                pl.BlockSpec((None, d, ff), lambda bb, i: (layer, 0, 1), pipeline_mode=single),
                pl.BlockSpec((None, ff, d), lambda bb, i: (layer, 0, 0), pipeline_mode=single),
                pl.BlockSpec((1, d), lambda bb, i: (0, 0))]
    args = [x, cond, cond, cond, norms, w13, w13, w2, final_g]
    if mixer_ys:
        in_specs.append(_cond_spec(lambda bb: cond_base(bb) + mod0 - 1, d))
        in_specs += [pl.BlockSpec((1, tm, y.shape[2]), lambda bb, i: (bb, i, 0)) for y in mixer_ys]
        in_specs.append(pl.BlockSpec((None,) + w_out.shape[1:], lambda bb, i: (layer, 0, 0),
                                     pipeline_mode=single))
        args += [cond, *mixer_ys, w_out]
    return pl.pallas_call(
        functools.partial(_ffn_body, apply_final=apply_final, mixer_parts=len(mixer_ys)),
        grid=(b, s // tm),
        in_specs=in_specs,
        out_specs=pl.BlockSpec((1, tm, d), lambda bb, i: (bb, i, 0)),
        out_shape=jax.ShapeDtypeStruct((b, s, d), F32),
        compiler_params=_cparams(2),
        name="ffn",
    )(*args)


N_PB = 3 * GROUP_WIDTH
N_PM = 3 * GROUP_WIDTH
N_PR = 5 * GROUP_WIDTH
N_PS = 3 * GROUP_WIDTH
N_PROJ = N_PB + N_PM + N_PR + N_PS
N_PT = 2 * GROUP_WIDTH


def _inproj_body(x_ref, sh_ref, sc_ref, ng_ref, w_ref, wt_ref,
                 pb_ref, pm_ref, pr_ref, ps_ref, pt_ref, gr_ref):
    sub = x_ref.shape[1] // INPROJ_SPLIT

    def row_group(r):
        rs = slice(r * sub, (r + 1) * sub)
        hb = _rms_mod(x_ref[0, rs, :], ng_ref[...], sh_ref[0], sc_ref[0]).astype(BF16)
        yield
        p = _dot(hb, w_ref[...])
        pt = _dot_nt(wt_ref[...], hb)
        yield
        o = 0
        pb_ref[0, rs, :] = p[:, o:o + N_PB].astype(BF16)
        o += N_PB
        pm_ref[0, rs, :] = p[:, o:o + N_PM]
        o += N_PM
        pr_ref[0, rs, :] = p[:, o:o + N_PR]
        o += N_PR
        ps_ref[0, rs, :] = p[:, o:o + N_PS]
        pt_ref[0, :, rs] = pt[0:N_PT, :].astype(BF16)
        gr_ref[0, :, rs] = pt[N_PT:N_PT + GATE_ROWS, :]

    _lockstep(row_group(r) for r in range(INPROJ_SPLIT))


def _inproj(x, cond, cond_base, norms, norm_row, w_all, w_t, layer):
    b, s, d = x.shape
    tm = INPROJ_ROWS
    single = pl.Buffered(1)

    def tok(width):
        return pl.BlockSpec((1, tm, width), lambda bb, i: (bb, i, 0))

    return pl.pallas_call(
        _inproj_body,
        grid=(b, s // tm),
        in_specs=[tok(d),
                  _cond_spec(lambda bb: cond_base(bb) + 3, d),
                  _cond_spec(lambda bb: cond_base(bb) + 4, d),
                  _layer_spec(norms, norm_row),
                  pl.BlockSpec((None, d, N_PROJ), lambda bb, i: (layer, 0, 0), pipeline_mode=single),
                  pl.BlockSpec((None, N_PT + GATE_ROWS, d), lambda bb, i: (layer, 0, 0))],
        out_specs=[tok(N_PB), tok(N_PM), tok(N_PR), tok(N_PS),
                   pl.BlockSpec((1, N_PT, tm), lambda bb, i: (bb, 0, i)),
                   pl.BlockSpec((1, GATE_ROWS, tm), lambda bb, i: (bb, 0, i))],
        out_shape=[jax.ShapeDtypeStruct((b, s, N_PB), BF16),
                   jax.ShapeDtypeStruct((b, s, N_PM), F32),
                   jax.ShapeDtypeStruct((b, s, N_PR), F32),
                   jax.ShapeDtypeStruct((b, s, N_PS), F32),
                   jax.ShapeDtypeStruct((b, N_PT, s), BF16),
                   jax.ShapeDtypeStruct((b, GATE_ROWS, s), F32)],
        compiler_params=_cparams(2),
        name="inproj",
    )(x, cond, cond, norms, w_all, w_t)


def _fox_body(k_ref, qt_ref, vt_ref, gr_ref, bcol_ref, y_ref, cum_c, cum_r, acc_s):
    nb, s_len = k_ref.shape[0], k_ref.shape[1]
    t = ATT_BLOCK
    w = GROUP_WIDTH
    nblk = s_len // t
    _, tri_u = _tri(t)
    key_le_query = (lax.broadcasted_iota(jnp.int32, (t, t), 0)
                    <= lax.broadcasted_iota(jnp.int32, (t, t), 1))
    row_head = lax.broadcasted_iota(jnp.int32, (w, t), 0) // HEAD_DIM
    rmask = [jnp.where(row_head == h, 1.0, 0.0).astype(BF16) for h in range(N_HEADS)]
    units = [(bi, h) for bi in range(nb) for h in range(N_HEADS)]
    acc_rows = acc_s.shape[2]
    ones_rows = jnp.ones((acc_rows - HEAD_DIM, t), BF16)

    def cum_step(c, carry):
        r0 = pl.multiple_of(c * t, t)
        out = []
        for bi in range(nb):
            lf_r = _log_sigmoid(gr_ref[bi, :, pl.ds(r0, t)] + bcol_ref[:, 0:1])
            cr = _cumsum_lanes(lf_r, tri_u) + carry[bi]
            cum_r[bi, :, pl.ds(r0, t)] = cr
            cum_c[bi, pl.ds(r0, t), :] = cr.T
            out.append(cr[:, t - 1:t])
        return tuple(out)

    lax.fori_loop(0, nblk, cum_step, tuple(jnp.zeros((GATE_ROWS, 1), F32) for _ in range(nb)))

    def q_step(i, _):
        q0 = pl.multiple_of(i * t, t)
        qts, cts = [], []
        for bi, h in units:
            qts.append(qt_ref[bi, :, pl.ds(q0, t)] * rmask[h])
            cts.append(cum_r[bi, G_FOX + h:G_FOX + h + 1, pl.ds(q0, t)])
        acc_s[...] = jnp.zeros((nb, N_HEADS, acc_rows, t), F32)

        def kv_block(j, maxes, diagonal):
            k0 = pl.multiple_of(j * t, t)
            kbs = [k_ref[bi, pl.ds(k0, t), :] for bi in range(nb)]
            scores = [_dot(kbs[bi], qts[u]) for u, (bi, h) in enumerate(units)]
            new_maxes, probs, alphas = [], [], []
            for u, (bi, h) in enumerate(units):
                cs = cum_c[bi, pl.ds(k0, t), G_FOX + h:G_FOX + h + 1]
                sc = scores[u] + cts[u] - cs
                if diagonal:
                    sc = jnp.where(key_le_query, sc, NEG_BIG)
                m_new = jnp.maximum(maxes[u], jnp.max(sc, axis=0, keepdims=True))
                alphas.append(jnp.exp(maxes[u] - m_new))
                probs.append(jnp.exp(sc - m_new).astype(BF16))
                new_maxes.append(m_new)
            for u, (bi, h) in enumerate(units):
                v_aug = jnp.concatenate(
                    [vt_ref[bi, h * HEAD_DIM:(h + 1) * HEAD_DIM, pl.ds(k0, t)], ones_rows], axis=0)
                acc_s[bi, h] = alphas[u] * acc_s[bi, h] + _dot(v_aug, probs[u])
            return tuple(new_maxes)

        init = tuple(jnp.full((1, t), NEG_BIG, F32) for _ in units)
        maxes = lax.fori_loop(0, i, lambda j, mx: kv_block(j, mx, False), init)
        kv_block(i, maxes, True)
        for bi in range(nb):
            out_t = jnp.concatenate(
                [acc_s[bi, h, 0:HEAD_DIM, :] / acc_s[bi, h, HEAD_DIM:HEAD_DIM + 1, :]
                 for h in range(N_HEADS)], axis=0)
            y_ref[bi, pl.ds(q0, t), :] = out_t.T.astype(BF16)
        return 0

    lax.fori_loop(0, nblk, q_step, 0)


def _fox(pb, pt, gr, bias_cols, layer):
    b, s, _ = pb.shape
    w = GROUP_WIDTH
    t = ATT_BLOCK
    nb = FOX_BATCH
    return pl.pallas_call(
        _fox_body,
        grid=(b // nb,),
        in_specs=[pl.BlockSpec((nb, s, w), lambda bb: (bb, 0, 0)),
                  pl.BlockSpec((nb, w, s), lambda bb: (bb, 0, 0)),
                  pl.BlockSpec((nb, w, s), lambda bb: (bb, 1, 0)),
                  pl.BlockSpec((nb, GATE_ROWS, s), lambda bb: (bb, 0, 0)),
                  _layer_spec(bias_cols, layer)],
        out_specs=pl.BlockSpec((nb, s, w), lambda bb: (bb, 0, 0)),
        out_shape=jax.ShapeDtypeStruct((b, s, w), BF16),
        scratch_shapes=[pltpu.VMEM((nb, s, GATE_ROWS), F32), pltpu.VMEM((nb, GATE_ROWS, s), F32),
                        pltpu.VMEM((nb, N_HEADS, HEAD_DIM + FOX_SUM_ROWS, t), F32)],
        compiler_params=_cparams(1),
        name="fox",
    )(pb, pt, pt, gr, bias_cols)


def _lockstep(gens):
    gens = list(gens)
    while gens:
        alive = []
        for g in gens:
            try:
                next(g)
                alive.append(g)
            except StopIteration:
                pass
        gens = alive


def _ret_body(p_ref, v_ref, lg_ref, inv_ref, sgn_ref, ng_ref, y_ref, cos_s, sin_s, state_s):
    nb, ts = p_ref.shape[0], p_ref.shape[1]
    n = CHUNK
    w = GROUP_WIDTH
    si = pl.program_id(1)
    base = pl.multiple_of(si * ts, ts)
    causal, _ = _tri(n)
    head_lane = _head_of_lane()
    hmask = _head_masks(head_lane)
    gmat = _group_mean_matrix()
    bd = _block_diag_mask(w)

    @pl.when(pl.program_id(0) == 0)
    def _():
        def tab(c, _):
            r0 = pl.multiple_of(base + c * n, n)
            pos = (lax.broadcasted_iota(jnp.int32, (n, 1), 0) + r0).astype(F32)
            ang = pos * inv_ref[...]
            cos_s[pl.ds(r0, n), :] = jnp.cos(ang)
            sin_s[pl.ds(r0, n), :] = jnp.sin(ang) * sgn_ref[...]
            return 0
        lax.fori_loop(0, ts // n, tab, 0)

    @pl.when(si == 0)
    def _():
        state_s[...] = jnp.zeros((nb, w, w), F32)

    tcol = (lax.broadcasted_iota(jnp.int32, (n, 1), 0) + 1).astype(F32)
    trow = (lax.broadcasted_iota(jnp.int32, (1, n), 1) + 1).astype(F32)
    lgs = [lg_ref[0:1, h:h + 1] for h in range(N_HEADS)]
    decays = [jnp.where(causal, jnp.exp(tcol * lgs[h] - trow * lgs[h]), 0.0) for h in range(N_HEADS)]
    q_scale = _expand([jnp.exp(tcol * lgs[h]) for h in range(N_HEADS)], head_lane)
    w_end = _expand([jnp.exp(n * lgs[h] - tcol * lgs[h]) for h in range(N_HEADS)], head_lane)
    a_chunk = _expand([jnp.exp(n * lgs[h]) for h in range(N_HEADS)], head_lane)

    def per_batch(bi, c):
        r0 = pl.multiple_of(c * n, n)
        rows = pl.ds(r0, n)
        trows = pl.ds(pl.multiple_of(base + r0, n), n)
        cos = cos_s[trows, :]
        sin = sin_s[trows, :]
        q = p_ref[bi, rows, 0:w] * cos + p_ref[bi, rows, w:2 * w] * sin
        k = p_ref[bi, rows, 2 * w:3 * w] * cos + p_ref[bi, rows, 3 * w:4 * w] * sin
        v = v_ref[bi, rows, :]
        qb = q.astype(BF16)
        kb = k.astype(BF16)
        qs = _dot(qb, state_s[bi].astype(BF16))
        qk = _dot_nt(jnp.concatenate([qb * hmask[h] for h in range(N_HEADS)], axis=0), kb)
        yield
        probs = [(qk[h * n:(h + 1) * n, :] * decays[h]).astype(BF16) for h in range(N_HEADS)]
        pv = _dot(jnp.concatenate(probs, axis=1),
                  jnp.concatenate([v * hmask[h] for h in range(N_HEADS)], axis=0))
        kv = _dot_tn((k * w_end).astype(BF16), v)
        yield
        y = qs * q_scale + pv
        state_s[bi] = state_s[bi] * a_chunk + jnp.where(bd, kv, 0.0)
        msq = _head_mean_sq(y, gmat)
        yield
        yn = y * lax.rsqrt(msq + RMS_EPS) * ng_ref[...]
        y_ref[bi, rows, :] = (_silu(p_ref[bi, rows, 4 * w:5 * w]) * yn).astype(BF16)

    def chunk(c, _):
        _lockstep(per_batch(bi, c) for bi in range(nb))
        return 0

    lax.fori_loop(0, ts // n, chunk, 0)


def _mixer_specs(widths_and_blocks):
    return [pl.BlockSpec((MIX_BATCH, MIX_ROWS, width), lambda bb, ss, blk=blk: (bb, ss, blk))
            for width, blk in widths_and_blocks]


def _mixer_const(shape):
    return pl.BlockSpec(shape, lambda bb, ss: (0,) * len(shape))


def _ret(pr, pb, lg_row, inv_row, sgn_row, norm_gs, layer):
    b, s, _ = pr.shape
    w = GROUP_WIDTH
    return pl.pallas_call(
        _ret_body,
        grid=(b // MIX_BATCH, s // MIX_ROWS),
        in_specs=_mixer_specs([(N_PR, 0), (w, 2)]) + [
            _mixer_const((1, LANES)), _mixer_const((1, w)), _mixer_const((1, w)),
            _layer_spec(norm_gs, layer)],
        out_specs=_mixer_specs([(w, 0)])[0],
        out_shape=jax.ShapeDtypeStruct((b, s, w), BF16),
        scratch_shapes=[pltpu.VMEM((s, w), F32), pltpu.VMEM((s, w), F32),
                        pltpu.VMEM((MIX_BATCH, w, w), F32)],
        compiler_params=_cparams(2),
        name="retention",
    )(pr, pb, lg_row, inv_row, sgn_row, norm_gs)


def _ssd_body(p_ref, gr_ref, cw_ref, cb_ref, bcol_ref, acol_ref, dskip_ref, ng_ref, y_ref,
              state_s, tail_s, x_s, q_s, k_s, v_s, vw_s, y_s, crow_s, ccol_s, ctmat_s, qscale_s,
              dtl_s, wend_s):
    nb, ts = p_ref.shape[0], p_ref.shape[1]
    n = CHUNK
    w = GROUP_WIDTH
    causal, _ = _tri(n)
    head_lane = _head_of_lane()
    hmask = _head_masks(head_lane)
    gw = SSD_GROUPS * HEAD_DIM
    hpg = N_HEADS // SSD_GROUPS
    group_lane = lax.broadcasted_iota(jnp.int32, (1, gw), 1) // HEAD_DIM
    gmask = [jnp.where(group_lane == g, 1.0, 0.0).astype(BF16) for g in range(SSD_GROUPS)]
    state_mask = (lax.broadcasted_iota(jnp.int32, (gw, w), 0) // HEAD_DIM
                  == lax.broadcasted_iota(jnp.int32, (gw, w), 1) // (HEAD_DIM * hpg))

    @pl.when(pl.program_id(1) == 0)
    def _():
        state_s[...] = jnp.zeros((nb, gw, w), F32)
        tail_s[...] = jnp.zeros((nb, 8, 2 * w), F32)

    batch = range(nb)
    n_chunks = ts // n
    half = G_DT - 8
    sel_row = lax.broadcasted_iota(jnp.int32, (GATE_ROWS, 1), 0) - half
    sel_chunk = jnp.where(
        sel_row == lax.broadcasted_iota(jnp.int32, (1, N_HEADS * n), 1) // n, 1.0, 0.0).astype(BF16)
    sel_head = jnp.where(sel_row == head_lane, 1.0, 0.0).astype(BF16)
    r_i = lax.broadcasted_iota(jnp.int32, (n, 2 * n), 0)
    c_i = lax.broadcasted_iota(jnp.int32, (n, 2 * n), 1)
    scan_mat = jnp.where(((c_i < n) & (r_i <= c_i)) | ((c_i >= n) & (r_i > c_i - n)), 1.0, 0.0
                         ).astype(BF16)
    dt16s = [_softplus(gr_ref[bi] + bcol_ref[:, 0:1]) for bi in batch]
    la_stack = jnp.concatenate([(dt16s[bi] * acol_ref[:, 0:1])[:, c * n:(c + 1) * n]
                                for bi in batch for c in range(n_chunks)], axis=0)
    scans = _cumsum_lanes(la_stack, scan_mat)

    def unstack(bi, lo):
        return jnp.concatenate(
            [scans[(bi * n_chunks + c) * GATE_ROWS + 8:(bi * n_chunks + c + 1) * GATE_ROWS, lo:lo + n]
             for c in range(n_chunks)], axis=1)

    cum8s = [unstack(bi, 0) for bi in batch]
    rev8s = [unstack(bi, n) for bi in batch]
    dt8s = [dt16[8:16, :] for dt16 in dt16s]
    ct_mats = [_rows_to_lanes(c8, sel_chunk) for c8 in cum8s]
    q_scales = [jnp.exp(_rows_to_lanes(c8, sel_head)) for c8 in cum8s]
    dt_lanes = [_rows_to_lanes(d8, sel_head) for d8 in dt8s]
    w_ends = [jnp.exp(_rows_to_lanes(r8, sel_head)) for r8 in rev8s]
    for bi in batch:
        crow_s[bi] = cum8s[bi]
        ccol_s[bi] = cum8s[bi].T
        for h in range(N_HEADS):
            ctmat_s[bi, h] = ct_mats[bi][:, h * n:(h + 1) * n]
        qscale_s[bi] = q_scales[bi]
        dtl_s[bi] = dt_lanes[bi]
        wend_s[bi] = w_ends[bi]

    def conv_step(c, _):
        rows = pl.ds(pl.multiple_of(c * n, n), n)
        for bi in batch:
            pre = p_ref[bi, rows, w:3 * w]
            xbc = _causal_conv_silu(pre, tail_s[bi], cw_ref, cb_ref)
            tail_s[bi] = pre[n - 8:n, :]
            x = xbc[:, 0:w]
            x_s[bi, rows, :] = x
            k_s[bi, rows, :] = xbc[:, w:w + gw].astype(BF16)
            q_s[bi, rows, :] = xbc[:, w + gw:w + 2 * gw].astype(BF16)
            v = x * dtl_s[bi, rows, :]
            v_s[bi, rows, :] = v.astype(BF16)
            vw_s[bi, rows, :] = (v * wend_s[bi, rows, :]).astype(BF16)
        return 0

    lax.fori_loop(0, n_chunks, conv_step, 0)

    def per_batch(bi, c):
        rows = pl.ds(pl.multiple_of(c * n, n), n)
        qb = q_s[bi, rows, :]
        kb = k_s[bi, rows, :]
        vb = v_s[bi, rows, :]
        qs = _dot(qb, state_s[bi].astype(BF16))
        qk = _dot_nt(jnp.concatenate([qb * gmask[g] for g in range(SSD_GROUPS)], axis=0), kb)
        kv = _dot_tn(kb, vw_s[bi, rows, :])
        yield
        c_rows = crow_s[bi, :, rows]
        probs = [(qk[(h // hpg) * n:(h // hpg + 1) * n, :]
                  * jnp.where(causal, jnp.exp(ctmat_s[bi, h, rows, :]
                                              - c_rows[half + h:half + h + 1, :]), 0.0)
                  ).astype(BF16) for h in range(N_HEADS)]
        pv = _dot(jnp.concatenate(probs, axis=1),
                  jnp.concatenate([vb * hmask[h] for h in range(N_HEADS)], axis=0))
        yield
        y_s[bi, rows, :] = qs * qscale_s[bi, rows, :] + pv
        c_end = ccol_s[bi, pl.ds(pl.multiple_of(c * n, n) + n - 8, 8), :][7:8, :]
        a_chunk = _expand([jnp.exp(c_end[:, half + h:half + h + 1]) for h in range(N_HEADS)],
                          head_lane)
        state_s[bi] = state_s[bi] * a_chunk + jnp.where(state_mask, kv, 0.0)

    def chunk(c, _):
        _lockstep(per_batch(bi, c) for bi in batch)
        return 0

    lax.fori_loop(0, n_chunks, chunk, 0)

    def norm_step(c, _):
        rows = pl.ds(pl.multiple_of(c * n, n), n)
        for bi in batch:
            hs = (y_s[bi, rows, :] + dskip_ref[...] * x_s[bi, rows, :]) * _silu(p_ref[bi, rows, 0:w])
            yn = hs * lax.rsqrt(jnp.mean(hs * hs, axis=-1, keepdims=True) + RMS_EPS) * ng_ref[...]
            y_ref[bi, rows, :] = yn.astype(BF16)
        return 0

    lax.fori_loop(0, n_chunks, norm_step, 0)


def _gate_row_spec():
    return pl.BlockSpec((MIX_BATCH, GATE_ROWS, MIX_ROWS), lambda bb, ss: (bb, 0, ss))


def _ssd(ps, gr, conv_ws, conv_bs, bias_cols, a_cols, d_rows, norm_gs, layer):
    b, s, _ = ps.shape
    w = GROUP_WIDTH
    gw = SSD_GROUPS * HEAD_DIM
    nb = MIX_BATCH
    params = (conv_ws, conv_bs, bias_cols, a_cols, d_rows, norm_gs)
    return pl.pallas_call(
        _ssd_body,
        grid=(b // MIX_BATCH, s // MIX_ROWS),
        in_specs=_mixer_specs([(N_PS, 0)]) + [_gate_row_spec()] + [
            _layer_spec(p, layer) for p in params],
        out_specs=_mixer_specs([(w, 0)])[0],
        out_shape=jax.ShapeDtypeStruct((b, s, w), BF16),
        scratch_shapes=[pltpu.VMEM((nb, gw, w), F32),
                        pltpu.VMEM((nb, 8, 2 * w), F32),
                        pltpu.VMEM((nb, MIX_ROWS, w), F32),
                        pltpu.VMEM((nb, MIX_ROWS, gw), BF16),
                        pltpu.VMEM((nb, MIX_ROWS, gw), BF16),
                        pltpu.VMEM((nb, MIX_ROWS, w), BF16),
                        pltpu.VMEM((nb, MIX_ROWS, w), BF16),
                        pltpu.VMEM((nb, MIX_ROWS, w), F32),
                        pltpu.VMEM((nb, 8, MIX_ROWS), F32),
                        pltpu.VMEM((nb, MIX_ROWS, 8), F32),
                        pltpu.VMEM((nb, N_HEADS, MIX_ROWS, CHUNK), F32),
                        pltpu.VMEM((nb, MIX_ROWS, w), F32),
                        pltpu.VMEM((nb, MIX_ROWS, w), F32),
                        pltpu.VMEM((nb, MIX_ROWS, w), F32)],
        compiler_params=_cparams(2),
        name="ssd",
    )(ps, gr, *params)


def _rows_to_lanes(x8, sel):
    parts = _split3(jnp.concatenate([x8, jnp.zeros_like(x8)], axis=0))
    return _dot_tn(jnp.concatenate(parts, axis=0), jnp.concatenate([sel, sel, sel], axis=0))


def _chunk_cummax(xs, n):
    lane_in_chunk = lax.broadcasted_iota(jnp.int32, (1, xs[0].shape[1]), 1) % n
    shift = 1
    while shift < n:
        keep = lane_in_chunk >= shift
        xs = [jnp.maximum(x, jnp.where(keep, pltpu.roll(x, shift, 1), NEG_BIG)) for x in xs]
        shift *= 2
    return xs


def _mlstm_body(p_ref, v_ref, gr_ref, cw_ref, cb_ref, bcol_ref, ng_ref, y_ref,
                state_s, m_s, tail_s, q_s, k_s, h_s, rrow_s, colb_s, colc_s, cmat_s, cexp_s,
                rexp_s):
    nb, ts = p_ref.shape[0], p_ref.shape[1]
    n = CHUNK
    w = GROUP_WIDTH
    wa = w + LANES
    causal, _ = _tri(n)
    head_lane = _head_of_lane()
    hmask = _head_masks(head_lane)
    head_lane_aug = jnp.concatenate(
        [head_lane, lax.broadcasted_iota(jnp.int32, (1, LANES), 1)], axis=1)
    gmat = _group_mean_matrix()
    bd_aug = _block_diag_mask(wa)
    lane128 = lax.broadcasted_iota(jnp.int32, (1, LANES), 1)
    ones_col = [jnp.where(lane128 == h, 1.0, 0.0).astype(BF16) for h in range(N_HEADS)]

    @pl.when(pl.program_id(1) == 0)
    def _():
        state_s[...] = jnp.zeros((nb, w, wa), F32)
        m_s[...] = jnp.zeros((nb, 1, 8), F32)
        tail_s[...] = jnp.zeros((nb, 8, 2 * w), F32)

    _, tri_u = _tri(n)
    sel_row = lax.broadcasted_iota(jnp.int32, (GATE_ROWS, 1), 0)
    sel_chunk = jnp.where(
        sel_row == lax.broadcasted_iota(jnp.int32, (1, N_HEADS * n), 1) // n, 1.0, 0.0).astype(BF16)
    sel_head = jnp.where(sel_row == head_lane, 1.0, 0.0).astype(BF16)
    batch = range(nb)
    n_chunks = ts // n
    g_blks = [gr_ref[bi] + bcol_ref[:, 0:1] for bi in batch]
    lf_blks = [_log_sigmoid(g) for g in g_blks]
    b_stack = _cumsum_lanes(
        jnp.concatenate([lf[:, c * n:(c + 1) * n] for lf in lf_blks for c in range(n_chunks)], axis=0),
        tri_u)
    b16s = [jnp.concatenate([b_stack[(bi * n_chunks + c) * GATE_ROWS:(bi * n_chunks + c + 1) * GATE_ROWS]
                             for c in range(n_chunks)], axis=1) for bi in batch]
    r16s = [pltpu.roll(g_blks[bi], G_MF - G_MI, 0) - b16s[bi] for bi in batch]
    b8s = [b16[G_MF:G_MF + 8, :] for b16 in b16s]
    r8s = [r16[G_MF:G_MF + 8, :] for r16 in r16s]
    c8s = _chunk_cummax(r8s, n)
    c_mats = [_rows_to_lanes(c8, sel_chunk) for c8 in c8s]
    c_exps = [_rows_to_lanes(c8, sel_head) for c8 in c8s]
    r_exps = [_rows_to_lanes(r8, sel_head) for r8 in r8s]
    for bi in batch:
        rrow_s[bi] = r8s[bi]
        colb_s[bi] = b8s[bi].T
        colc_s[bi] = c8s[bi].T
        for h in range(N_HEADS):
            cmat_s[bi, h] = c_mats[bi][:, h * n:(h + 1) * n]
        cexp_s[bi] = c_exps[bi]
        rexp_s[bi] = r_exps[bi]
    def conv_step(c, _):
        rows = pl.ds(pl.multiple_of(c * n, n), n)
        for bi in batch:
            pre = p_ref[bi, rows, 0:2 * w]
            qk_act = _causal_conv_silu(pre, tail_s[bi], cw_ref, cb_ref)
            tail_s[bi] = pre[n - 8:n, :]
            q_s[bi, rows, :] = (qk_act[:, 0:w] * (HEAD_DIM ** -0.5)).astype(BF16)
            k_s[bi, rows, :] = qk_act[:, w:2 * w]
        return 0

    lax.fori_loop(0, n_chunks, conv_step, 0)

    def per_batch(bi, c):
        r0 = pl.multiple_of(c * n, n)
        rows = pl.ds(r0, n)
        qb = q_s[bi, rows, :]
        k = k_s[bi, rows, :]
        kb = k.astype(BF16)
        vb = v_ref[bi, rows, :]

        qs = _dot(qb, state_s[bi].astype(BF16))
        qk = _dot_nt(jnp.concatenate([qb * hmask[h] for h in range(N_HEADS)], axis=0), kb)
        yield

        heads = range(N_HEADS)
        m8 = m_s[bi]
        m_ins = [m8[:, h:h + 1] for h in heads]
        r_rows = rrow_s[bi, :, rows]
        col_b = colb_s[bi, rows, :]
        col_c = colc_s[bi, rows, :]
        c_end8 = col_c[n - 1:n, :]
        z_end8 = jnp.maximum(m8, c_end8)
        probs = [(jnp.where(causal,
                            jnp.exp(r_rows[h:h + 1, :] - jnp.maximum(m_ins[h], cmat_s[bi, h, rows, :])),
                            0.0) * qk[h * n:(h + 1) * n, :]).astype(BF16) for h in heads]
        v_rows = jnp.concatenate(
            [jnp.concatenate([vb * hmask[h], jnp.broadcast_to(ones_col[h], (n, LANES))], axis=1)
             for h in heads], axis=0)
        pv = _dot(jnp.concatenate(probs, axis=1), v_rows)
        c_ends = [c_end8[:, h:h + 1] for h in heads]
        kw = (k * jnp.exp(rexp_s[bi, rows, :] - _expand(c_ends, head_lane))).astype(BF16)
        v_aug = jnp.concatenate([vb, jnp.ones((n, LANES), BF16)], axis=1)
        kv = _dot_tn(kw, v_aug)
        yield
        m_lanes = _expand(m_ins, head_lane)
        a_inter = jnp.exp(m_lanes - jnp.maximum(m_lanes, cexp_s[bi, rows, :]))
        num = pv[:, 0:w] + a_inter * qs[:, 0:w]
        z8 = jnp.maximum(m8, col_c)
        den8 = pv[:, w:w + 8] + jnp.exp(m8 - z8) * qs[:, w:w + 8]
        dens8 = jnp.maximum(jnp.abs(den8), jnp.exp(-(col_b + z8)))
        h_s[bi, rows, :] = num / _expand([dens8[:, h:h + 1] for h in heads], head_lane)
        a_old8 = jnp.exp(m8 - z_end8)
        a_loc8 = jnp.exp(c_end8 - z_end8)
        state_s[bi] = (state_s[bi] * _expand([a_old8[:, h:h + 1] for h in heads], head_lane_aug)
                       + jnp.where(bd_aug, kv, 0.0)
                       * _expand([a_loc8[:, h:h + 1] for h in heads], head_lane_aug))
        m_s[bi] = col_b[n - 1:n, :] + z_end8

    def chunk(c, _):
        _lockstep(per_batch(bi, c) for bi in range(nb))
        return 0

    lax.fori_loop(0, ts // n, chunk, 0)

    def norm_step(c, _):
        rows = pl.ds(pl.multiple_of(c * n, n), n)
        hvals = [h_s[bi, rows, :] for bi in batch]
        msqs = [_head_mean_sq(hv, gmat) for hv in hvals]
        for bi in batch:
            yn = hvals[bi] * lax.rsqrt(msqs[bi] + RMS_EPS) * ng_ref[...]
            y_ref[bi, rows, :] = (_sigmoid(p_ref[bi, rows, 2 * w:3 * w]) * yn).astype(BF16)
        return 0

    lax.fori_loop(0, n_chunks, norm_step, 0)


def _mlstm(pm, pb, gr, conv_ws, conv_bs, bias_cols, norm_gs, layer):
    b, s, _ = pm.shape
    w = GROUP_WIDTH
    params = (conv_ws, conv_bs, bias_cols, norm_gs)
    return pl.pallas_call(
        _mlstm_body,
        grid=(b // MIX_BATCH, s // MIX_ROWS),
        in_specs=_mixer_specs([(N_PM, 0), (w, 1)]) + [_gate_row_spec()] + [
            _layer_spec(p, layer) for p in params],
        out_specs=_mixer_specs([(w, 0)])[0],
        out_shape=jax.ShapeDtypeStruct((b, s, w), BF16),
        scratch_shapes=[pltpu.VMEM((MIX_BATCH, w, w + LANES), F32),
                        pltpu.VMEM((MIX_BATCH, 1, 8), F32),
                        pltpu.VMEM((MIX_BATCH, 8, 2 * w), F32),
                        pltpu.VMEM((MIX_BATCH, MIX_ROWS, w), BF16),
                        pltpu.VMEM((MIX_BATCH, MIX_ROWS, w), F32),
                        pltpu.VMEM((MIX_BATCH, MIX_ROWS, w), F32),
                        pltpu.VMEM((MIX_BATCH, 8, MIX_ROWS), F32),
                        pltpu.VMEM((MIX_BATCH, MIX_ROWS, 8), F32),
                        pltpu.VMEM((MIX_BATCH, MIX_ROWS, 8), F32),
                        pltpu.VMEM((MIX_BATCH, N_HEADS, MIX_ROWS, CHUNK), F32),
                        pltpu.VMEM((MIX_BATCH, MIX_ROWS, w), F32),
                        pltpu.VMEM((MIX_BATCH, MIX_ROWS, w), F32)],
        compiler_params=_cparams(2),
        name="mlstm",
    )(pm, pb, gr, *params)


def _half_swap(t):
    lead = t.shape[:-1]
    t = t.reshape(lead + (N_HEADS, 2, HEAD_DIM // 2))
    return t[..., ::-1, :].reshape(lead + (GROUP_WIDTH,))


def _inproj_weights(w_in):
    w, h, hd = GROUP_WIDTH, N_HEADS, HEAD_DIM
    w_f32 = w_in
    w_in = w_in.astype(BF16)
    qk_scale = hd ** -0.5
    fox0 = 0
    ml0 = fox0 + 3 * w + h
    ret0 = ml0 + 4 * w + 2 * h
    ssd0 = ret0 + 4 * w
    bc = 2 * SSD_GROUPS * hd

    def col(start, width, src=None):
        return (w_in if src is None else src)[:, :, start:start + width]

    rq, rk = col(ret0, w), col(ret0 + w, w) * qk_scale
    gates = jnp.concatenate([col(fox0 + 3 * w, h, w_f32), col(ml0 + 3 * w, 2 * h, w_f32),
                             col(ssd0 + 2 * w + bc, h, w_f32)], axis=2)
    w_t = jnp.concatenate([col(fox0, w, w_f32) * qk_scale, col(fox0 + 2 * w, w, w_f32), gates],
                          axis=2)
    w_all = jnp.concatenate([
        col(fox0 + w, w), col(ml0 + 2 * w, w), col(ret0 + 2 * w, w),
        col(ml0, 2 * w), col(ml0 + 3 * w + 2 * h, w),
        rq, _half_swap(rq), rk, _half_swap(rk), col(ret0 + 3 * w, w),
        col(ssd0, w), col(ssd0 + w, w + bc)], axis=2)
    return w_all, jnp.swapaxes(w_t, 1, 2).astype(BF16)


def _lane_pad(v, width=LANES):
    return jnp.zeros((width,), F32).at[:v.shape[0]].set(v)


def kernel(x, c, ada_w, ada_b, norm_g, ffn1_w13, ffn1_w2, ffn2_w13, ffn2_w2, w_in, w_out, fox_fb,
           mlstm_conv_w, mlstm_conv_b, mlstm_ib, mlstm_fb, mlstm_norm_g, ret_norm_g, ssd_conv_w,
           ssd_conv_b, ssd_dt_bias, ssd_A_log, ssd_D, ssd_norm_g, final_g):
    n_layers = ada_w.shape[0]
    b, s, d = x.shape
    hd = HEAD_DIM

    cond = _adaln(c, ada_w, ada_b).reshape(n_layers * b * N_MOD, 1, d)

    inv = 1.0 / (ROPE_BASE ** (jnp.arange(0, hd, 2, dtype=F32) / hd))
    inv_row = jnp.tile(inv, 2 * N_HEADS).reshape(1, GROUP_WIDTH)
    sgn_row = jnp.tile(jnp.concatenate([-jnp.ones(hd // 2, F32), jnp.ones(hd // 2, F32)]),
                       N_HEADS).reshape(1, GROUP_WIDTH)
    log_gamma = jnp.log(1.0 - 2.0 ** (-RET_DECAY_OFFSET - jnp.arange(N_HEADS, dtype=F32)))
    lg_row = _lane_pad(log_gamma).reshape(1, LANES)

    w13_a, w2_a = ffn1_w13.astype(BF16), ffn1_w2.astype(BF16)
    w13_b, w2_b = ffn2_w13.astype(BF16), ffn2_w2.astype(BF16)
    w_out_b = w_out.astype(BF16)
    w_all, w_t = _inproj_weights(w_in)

    norms = norm_g.reshape(n_layers * 3, 1, d)
    final_row = final_g.reshape(1, d)
    gate_bias = jnp.concatenate([fox_fb, mlstm_ib, mlstm_fb, ssd_dt_bias], axis=1)
    bias_cols = jnp.broadcast_to(gate_bias[:, :, None], (n_layers, GATE_ROWS, LANES))
    a_rows = jnp.zeros((n_layers, GATE_ROWS), F32).at[:, G_DT:G_DT + N_HEADS].set(
        -jnp.exp(ssd_A_log.astype(F32)))
    a_cols = jnp.broadcast_to(a_rows[:, :, None], (n_layers, GATE_ROWS, LANES))
    d_rows = jnp.repeat(ssd_D, hd, axis=1).reshape(n_layers, 1, GROUP_WIDTH)
    mlstm_cb = mlstm_conv_b.reshape(n_layers, 1, -1)
    ssd_cb = ssd_conv_b.reshape(n_layers, 1, -1)
    mlstm_ng = mlstm_norm_g.reshape(n_layers, 1, GROUP_WIDTH)
    ret_ng = ret_norm_g.reshape(n_layers, 1, GROUP_WIDTH)
    ssd_ng = ssd_norm_g.reshape(n_layers, 1, GROUP_WIDTH)

    for l in range(n_layers):
        def cond_base(bb, l=l):
            return (l * b + bb) * N_MOD

        x = _ffn(x, cond, cond_base, 0, norms, 3 * l, w13_a, w2_a, l, final_row, False)
        pb, pm, pr, ps, pt, gr = _inproj(x, cond, cond_base, norms, 3 * l + 1, w_all, w_t, l)

        y_fox = _fox(pb, pt, gr, bias_cols, l)
        y_mlstm = _mlstm(pm, pb, gr, mlstm_conv_w, mlstm_cb, bias_cols, mlstm_ng, l)
        y_ret = _ret(pr, pb, lg_row, inv_row, sgn_row, ret_ng, l)
        y_ssd = _ssd(ps, gr, ssd_conv_w, ssd_cb, bias_cols, a_cols, d_rows, ssd_ng, l)

        x = _ffn(x, cond, cond_base, 6, norms, 3 * l + 2, w13_b, w2_b, l, final_row,
                 l == n_layers - 1, mixer_ys=(y_fox, y_mlstm, y_ret, y_ssd), w_out=w_out_b)
    return x
```

```python
import functools

import jax
import jax.numpy as jnp
from jax import lax
from jax.experimental import pallas as pl
from jax.experimental.pallas import tpu as pltpu

F32 = jnp.float32
BF16 = jnp.bfloat16

N_HEADS = 4
HEAD_DIM = 64
GROUP_WIDTH = N_HEADS * HEAD_DIM
CONV_K = 4
SSD_GROUPS = 2
N_MOD = 9
ROPE_BASE = 10000.0
RET_DECAY_OFFSET = 5.0
RMS_EPS = 1e-6
NEG_BIG = -1e30

VMEM_LIMIT_BYTES = 56 * 1024 * 1024
LANES = 128
GATE_ROWS = 16

G_FOX, G_MI, G_MF, G_DT = 0, 4, 8, 12

FFN_ROWS = 1024
FFN_SPLIT = 4
INPROJ_ROWS = 1024
INPROJ_SPLIT = 4
ATT_BLOCK = 256
FOX_SUM_ROWS = 16
FOX_BATCH = 4
CHUNK = 128
MIX_BATCH = 4
MIX_ROWS = 512


def _cparams(n_axes):
    return pltpu.CompilerParams(dimension_semantics=("arbitrary",) * n_axes,
                                vmem_limit_bytes=VMEM_LIMIT_BYTES)


def _sigmoid(x):
    return 1.0 / (1.0 + jnp.exp(-x))


def _silu(x):
    return x * _sigmoid(x)


def _softplus(x):
    return jnp.maximum(x, 0.0) + jnp.log1p(jnp.exp(-jnp.abs(x)))


def _log_sigmoid(x):
    return -_softplus(-x)


def _rms_mod(x, g, sh, sc):
    y = x * lax.rsqrt(jnp.mean(x * x, axis=-1, keepdims=True) + RMS_EPS) * g
    return y * (1.0 + sc) + sh


def _dot(a, b):
    return jnp.dot(a, b, preferred_element_type=F32)


def _dot_nt(a, b):
    return lax.dot_general(a, b, (((1,), (1,)), ((), ())), preferred_element_type=F32)


def _dot_tn(a, b):
    return lax.dot_general(a, b, (((0,), (0,)), ((), ())), preferred_element_type=F32)


def _split3(x):
    hi = x.astype(BF16)
    r1 = x - hi.astype(F32)
    mid = r1.astype(BF16)
    lo = (r1 - mid.astype(F32)).astype(BF16)
    return hi, mid, lo


def _cumsum_lanes(x, tri_upper):
    return _dot(jnp.concatenate(_split3(x), axis=1),
                jnp.concatenate([tri_upper, tri_upper, tri_upper], axis=0))


def _tri(n):
    r = lax.broadcasted_iota(jnp.int32, (n, n), 0)
    c = lax.broadcasted_iota(jnp.int32, (n, n), 1)
    return r >= c, jnp.where(r <= c, 1.0, 0.0).astype(BF16)


def _head_of_lane(width=GROUP_WIDTH):
    return lax.broadcasted_iota(jnp.int32, (1, width), 1) // HEAD_DIM


def _head_masks(head_lane):
    return [jnp.where(head_lane == h, 1.0, 0.0).astype(BF16) for h in range(N_HEADS)]


def _expand(vals, head_lane):
    out = vals[N_HEADS - 1]
    for h in range(N_HEADS - 2, -1, -1):
        out = jnp.where(head_lane == h, vals[h], out)
    return out


def _head_mean_sq(y, gmat):
    return _dot(jnp.concatenate(_split3(y * y), axis=1), jnp.concatenate([gmat, gmat, gmat], axis=0))


def _group_mean_matrix():
    r = lax.broadcasted_iota(jnp.int32, (GROUP_WIDTH, GROUP_WIDTH), 0) // HEAD_DIM
    c = lax.broadcasted_iota(jnp.int32, (GROUP_WIDTH, GROUP_WIDTH), 1) // HEAD_DIM
    return jnp.where(r == c, 1.0 / HEAD_DIM, 0.0).astype(BF16)


def _block_diag_mask(cols):
    r = lax.broadcasted_iota(jnp.int32, (GROUP_WIDTH, cols), 0) // HEAD_DIM
    c = lax.broadcasted_iota(jnp.int32, (GROUP_WIDTH, cols), 1)
    same = (c < GROUP_WIDTH) & (r == c // HEAD_DIM)
    if cols > GROUP_WIDTH:
        same = same | ((c >= GROUP_WIDTH) & (r == c - GROUP_WIDTH))
    return same


def _causal_conv_silu(x, prev_tail, w_ref, b_ref):
    n = x.shape[0]
    row8 = lax.broadcasted_iota(jnp.int32, (8, 1), 0)
    x8 = x[0:8, :]
    acc = x * w_ref[CONV_K - 1:CONV_K, :] + b_ref[...]
    acc8 = acc[0:8, :]
    for j in range(1, CONV_K):
        wj = w_ref[CONV_K - 1 - j:CONV_K - j, :]
        acc = acc + pltpu.roll(x, j, 0) * wj
        acc8 = acc8 + jnp.where(row8 < j, pltpu.roll(prev_tail, j, 0), pltpu.roll(x8, j, 0)) * wj
    return _silu(jnp.concatenate([acc8, acc[8:n, :]], axis=0))


def _adaln_body(c_ref, w_ref, b_ref, o_ref):
    c = c_ref[...]
    o_ref[0] = _dot(_silu(c).astype(BF16), w_ref[0].astype(BF16)) + b_ref[0]


def _adaln(c, ada_w, ada_b):
    n_layers, d, n_out = ada_w.shape
    b = c.shape[0]
    tn = 1152
    return pl.pallas_call(
        _adaln_body,
        grid=(n_layers, n_out // tn),
        in_specs=[pl.BlockSpec((b, d), lambda l, j: (0, 0)),
                  pl.BlockSpec((1, d, tn), lambda l, j: (l, 0, j)),
                  pl.BlockSpec((1, 1, tn), lambda l, j: (l, 0, j))],
        out_specs=pl.BlockSpec((1, b, tn), lambda l, j: (l, 0, j)),
        out_shape=jax.ShapeDtypeStruct((n_layers, b, n_out), F32),
        compiler_params=_cparams(2),
        name="adaln",
    )(c, ada_w, ada_b.reshape(n_layers, 1, n_out))


def _layer_spec(stack, row):
    return pl.BlockSpec((None,) + stack.shape[1:], lambda *_: (row,) + (0,) * (stack.ndim - 1))


def _cond_spec(cond_row, d):
    return pl.BlockSpec((1, 1, d), lambda b, i: (cond_row(b), 0, 0))


def _ffn_body(x_ref, sh_ref, sc_ref, gt_ref, ng_ref, w1_ref, w3_ref, w2_ref, fg_ref, *rest,
              apply_final, mixer_parts):
    o_ref = rest[-1]
    tm = x_ref.shape[1]
    sub = tm // FFN_SPLIT

    def row_group(r):
        rs = slice(r * sub, (r + 1) * sub)
        x = x_ref[0, rs, :]
        if mixer_parts:
            mg_ref, wo_ref = rest[0], rest[1 + mixer_parts]
            w = wo_ref.shape[0] // mixer_parts
            proj = _dot(rest[1][0, rs, :], wo_ref[0:w, :])
            for part in range(1, mixer_parts):
                proj += _dot(rest[1 + part][0, rs, :], wo_ref[part * w:(part + 1) * w, :])
            yield
            x = x + mg_ref[0] * proj
        hb = _rms_mod(x, ng_ref[...], sh_ref[0], sc_ref[0]).astype(BF16)
        yield
        a = _dot(hb, w1_ref[...])
        g = _dot(hb, w3_ref[...])
        yield
        act = (_silu(g) * a).astype(BF16)
        yield
        down = _dot(act, w2_ref[...])
        yield
        y = x + (0.5 * gt_ref[0]) * down
        if apply_final:
            y = y * lax.rsqrt(jnp.mean(y * y, axis=-1, keepdims=True) + RMS_EPS) * fg_ref[...]
        o_ref[0, rs, :] = y

    _lockstep(row_group(r) for r in range(FFN_SPLIT))


def _ffn(x, cond, cond_base, mod0, norms, norm_row, w13, w2, layer, final_g, apply_final,
         mixer_ys=(), w_out=None):
    b, s, d = x.shape
    ff = w2.shape[1]
    tm = FFN_ROWS
    single = pl.Buffered(1)
    in_specs = [pl.BlockSpec((1, tm, d), lambda bb, i: (bb, i, 0)),
                _cond_spec(lambda bb: cond_base(bb) + mod0, d),
                _cond_spec(lambda bb: cond_base(bb) + mod0 + 1, d),
                _cond_spec(lambda bb: cond_base(bb) + mod0 + 2, d),
                _layer_spec(norms, norm_row),
                pl.BlockSpec((None, d, ff), lambda bb, i: (layer, 0, 0), pipeline_mode=single),
                pl.BlockSpec((None, d, ff), lambda bb, i: (layer, 0, 1), pipeline_mode=single),
                pl.BlockSpec((None, ff, d), lambda bb, i: (layer, 0, 0), pipeline_mode=single),
                pl.BlockSpec((1, d), lambda bb, i: (0, 0))]
    args = [x, cond, cond, cond, norms, w13, w13, w2, final_g]
    if mixer_ys:
        in_specs.append(_cond_spec(lambda bb: cond_base(bb) + mod0 - 1, d))
        in_specs += [pl.BlockSpec((1, tm, y.shape[2]), lambda bb, i: (bb, i, 0)) for y in mixer_ys]
        in_specs.append(pl.BlockSpec((None,) + w_out.shape[1:], lambda bb, i: (layer, 0, 0),
                                     pipeline_mode=single))
        args += [cond, *mixer_ys, w_out]
    return pl.pallas_call(
        functools.partial(_ffn_body, apply_final=apply_final, mixer_parts=len(mixer_ys)),
        grid=(b, s // tm),
        in_specs=in_specs,
        out_specs=pl.BlockSpec((1, tm, d), lambda bb, i: (bb, i, 0)),
        out_shape=jax.ShapeDtypeStruct((b, s, d), F32),
        compiler_params=_cparams(2),
        name="ffn",
    )(*args)


N_PB = 3 * GROUP_WIDTH
N_PM = 3 * GROUP_WIDTH
N_PR = 5 * GROUP_WIDTH
N_PS = 3 * GROUP_WIDTH
N_PROJ = N_PB + N_PM + N_PR + N_PS
N_PT = 2 * GROUP_WIDTH


def _inproj_body(x_ref, sh_ref, sc_ref, ng_ref, w_ref, wt_ref,
                 pb_ref, pm_ref, pr_ref, ps_ref, pt_ref, gr_ref):
    sub = x_ref.shape[1] // INPROJ_SPLIT

    def row_group(r):
        rs = slice(r * sub, (r + 1) * sub)
        hb = _rms_mod(x_ref[0, rs, :], ng_ref[...], sh_ref[0], sc_ref[0]).astype(BF16)
        yield
        p = _dot(hb, w_ref[...])
        pt = _dot_nt(wt_ref[...], hb)
        yield
        o = 0
        pb_ref[0, rs, :] = p[:, o:o + N_PB].astype(BF16)
        o += N_PB
        pm_ref[0, rs, :] = p[:, o:o + N_PM]
        o += N_PM
        pr_ref[0, rs, :] = p[:, o:o + N_PR]
        o += N_PR
        ps_ref[0, rs, :] = p[:, o:o + N_PS]
        pt_ref[0, :, rs] = pt[0:N_PT, :].astype(BF16)
        gr_ref[0, :, rs] = pt[N_PT:N_PT + GATE_ROWS, :]

    _lockstep(row_group(r) for r in range(INPROJ_SPLIT))


def _inproj(x, cond, cond_base, norms, norm_row, w_all, w_t, layer):
    b, s, d = x.shape
    tm = INPROJ_ROWS
    single = pl.Buffered(1)

    def tok(width):
        return pl.BlockSpec((1, tm, width), lambda bb, i: (bb, i, 0))

    return pl.pallas_call(
        _inproj_body,
        grid=(b, s // tm),
        in_specs=[tok(d),
                  _cond_spec(lambda bb: cond_base(bb) + 3, d),
                  _cond_spec(lambda bb: cond_base(bb) + 4, d),
                  _layer_spec(norms, norm_row),
                  pl.BlockSpec((None, d, N_PROJ), lambda bb, i: (layer, 0, 0), pipeline_mode=single),
                  pl.BlockSpec((None, N_PT + GATE_ROWS, d), lambda bb, i: (layer, 0, 0))],
        out_specs=[tok(N_PB), tok(N_PM), tok(N_PR), tok(N_PS),
                   pl.BlockSpec((1, N_PT, tm), lambda bb, i: (bb, 0, i)),
                   pl.BlockSpec((1, GATE_ROWS, tm), lambda bb, i: (bb, 0, i))],
        out_shape=[jax.ShapeDtypeStruct((b, s, N_PB), BF16),
                   jax.ShapeDtypeStruct((b, s, N_PM), F32),
                   jax.ShapeDtypeStruct((b, s, N_PR), F32),
                   jax.ShapeDtypeStruct((b, s, N_PS), F32),
                   jax.ShapeDtypeStruct((b, N_PT, s), BF16),
                   jax.ShapeDtypeStruct((b, GATE_ROWS, s), F32)],
        compiler_params=_cparams(2),
        name="inproj",
    )(x, cond, cond, norms, w_all, w_t)


def _fox_body(k_ref, qt_ref, vt_ref, gr_ref, bcol_ref, y_ref, cum_c, cum_r, acc_s):
    nb, s_len = k_ref.shape[0], k_ref.shape[1]
    t = ATT_BLOCK
    w = GROUP_WIDTH
    nblk = s_len // t
    _, tri_u = _tri(t)
    key_le_query = (lax.broadcasted_iota(jnp.int32, (t, t), 0)
                    <= lax.broadcasted_iota(jnp.int32, (t, t), 1))
    row_head = lax.broadcasted_iota(jnp.int32, (w, t), 0) // HEAD_DIM
    rmask = [jnp.where(row_head == h, 1.0, 0.0).astype(BF16) for h in range(N_HEADS)]
    units = [(bi, h) for bi in range(nb) for h in range(N_HEADS)]
    acc_rows = acc_s.shape[2]
    ones_rows = jnp.ones((acc_rows - HEAD_DIM, t), BF16)

    def cum_step(c, carry):
        r0 = pl.multiple_of(c * t, t)
        out = []
        for bi in range(nb):
            lf_r = _log_sigmoid(gr_ref[bi, :, pl.ds(r0, t)] + bcol_ref[:, 0:1])
            cr = _cumsum_lanes(lf_r, tri_u) + carry[bi]
            cum_r[bi, :, pl.ds(r0, t)] = cr
            cum_c[bi, pl.ds(r0, t), :] = cr.T
            out.append(cr[:, t - 1:t])
        return tuple(out)

    lax.fori_loop(0, nblk, cum_step, tuple(jnp.zeros((GATE_ROWS, 1), F32) for _ in range(nb)))

    def q_step(i, _):
        q0 = pl.multiple_of(i * t, t)
        qts, cts = [], []
        for bi, h in units:
            qts.append(qt_ref[bi, :, pl.ds(q0, t)] * rmask[h])
            cts.append(cum_r[bi, G_FOX + h:G_FOX + h + 1, pl.ds(q0, t)])
        acc_s[...] = jnp.zeros((nb, N_HEADS, acc_rows, t), F32)

        def kv_block(j, maxes, diagonal):
            k0 = pl.multiple_of(j * t, t)
            kbs = [k_ref[bi, pl.ds(k0, t), :] for bi in range(nb)]
            scores = [_dot(kbs[bi], qts[u]) for u, (bi, h) in enumerate(units)]
            new_maxes, probs, alphas = [], [], []
            for u, (bi, h) in enumerate(units):
                cs = cum_c[bi, pl.ds(k0, t), G_FOX + h:G_FOX + h + 1]
                sc = scores[u] + cts[u] - cs
                if diagonal:
                    sc = jnp.where(key_le_query, sc, NEG_BIG)
                m_new = jnp.maximum(maxes[u], jnp.max(sc, axis=0, keepdims=True))
                alphas.append(jnp.exp(maxes[u] - m_new))
                probs.append(jnp.exp(sc - m_new).astype(BF16))
                new_maxes.append(m_new)
            for u, (bi, h) in enumerate(units):
                v_aug = jnp.concatenate(
                    [vt_ref[bi, h * HEAD_DIM:(h + 1) * HEAD_DIM, pl.ds(k0, t)], ones_rows], axis=0)
                acc_s[bi, h] = alphas[u] * acc_s[bi, h] + _dot(v_aug, probs[u])
            return tuple(new_maxes)

        init = tuple(jnp.full((1, t), NEG_BIG, F32) for _ in units)
        maxes = lax.fori_loop(0, i, lambda j, mx: kv_block(j, mx, False), init)
        kv_block(i, maxes, True)
        for bi in range(nb):
            out_t = jnp.concatenate(
                [acc_s[bi, h, 0:HEAD_DIM, :] / acc_s[bi, h, HEAD_DIM:HEAD_DIM + 1, :]
                 for h in range(N_HEADS)], axis=0)
            y_ref[bi, pl.ds(q0, t), :] = out_t.T.astype(BF16)
        return 0

    lax.fori_loop(0, nblk, q_step, 0)


def _fox(pb, pt, gr, bias_cols, layer):
    b, s, _ = pb.shape
    w = GROUP_WIDTH
    t = ATT_BLOCK
    nb = FOX_BATCH
    return pl.pallas_call(
        _fox_body,
        grid=(b // nb,),
        in_specs=[pl.BlockSpec((nb, s, w), lambda bb: (bb, 0, 0)),
                  pl.BlockSpec((nb, w, s), lambda bb: (bb, 0, 0)),
                  pl.BlockSpec((nb, w, s), lambda bb: (bb, 1, 0)),
                  pl.BlockSpec((nb, GATE_ROWS, s), lambda bb: (bb, 0, 0)),
                  _layer_spec(bias_cols, layer)],
        out_specs=pl.BlockSpec((nb, s, w), lambda bb: (bb, 0, 0)),
        out_shape=jax.ShapeDtypeStruct((b, s, w), BF16),
        scratch_shapes=[pltpu.VMEM((nb, s, GATE_ROWS), F32), pltpu.VMEM((nb, GATE_ROWS, s), F32),
                        pltpu.VMEM((nb, N_HEADS, HEAD_DIM + FOX_SUM_ROWS, t), F32)],
        compiler_params=_cparams(1),
        name="fox",
    )(pb, pt, pt, gr, bias_cols)


def _lockstep(gens):
    gens = list(gens)
    while gens:
        alive = []
        for g in gens:
            try:
                next(g)
                alive.append(g)
            except StopIteration:
                pass
        gens = alive


def _ret_body(p_ref, v_ref, lg_ref, inv_ref, sgn_ref, ng_ref, y_ref, cos_s, sin_s, state_s):
    nb, ts = p_ref.shape[0], p_ref.shape[1]
    n = CHUNK
    w = GROUP_WIDTH
    si = pl.program_id(1)
    base = pl.multiple_of(si * ts, ts)
    causal, _ = _tri(n)
    head_lane = _head_of_lane()
    hmask = _head_masks(head_lane)
    gmat = _group_mean_matrix()
    bd = _block_diag_mask(w)

    @pl.when(pl.program_id(0) == 0)
    def _():
        def tab(c, _):
            r0 = pl.multiple_of(base + c * n, n)
            pos = (lax.broadcasted_iota(jnp.int32, (n, 1), 0) + r0).astype(F32)
            ang = pos * inv_ref[...]
            cos_s[pl.ds(r0, n), :] = jnp.cos(ang)
            sin_s[pl.ds(r0, n), :] = jnp.sin(ang) * sgn_ref[...]
            return 0
        lax.fori_loop(0, ts // n, tab, 0)

    @pl.when(si == 0)
    def _():
        state_s[...] = jnp.zeros((nb, w, w), F32)

    tcol = (lax.broadcasted_iota(jnp.int32, (n, 1), 0) + 1).astype(F32)
    trow = (lax.broadcasted_iota(jnp.int32, (1, n), 1) + 1).astype(F32)
    lgs = [lg_ref[0:1, h:h + 1] for h in range(N_HEADS)]
    decays = [jnp.where(causal, jnp.exp(tcol * lgs[h] - trow * lgs[h]), 0.0) for h in range(N_HEADS)]
    q_scale = _expand([jnp.exp(tcol * lgs[h]) for h in range(N_HEADS)], head_lane)
    w_end = _expand([jnp.exp(n * lgs[h] - tcol * lgs[h]) for h in range(N_HEADS)], head_lane)
    a_chunk = _expand([jnp.exp(n * lgs[h]) for h in range(N_HEADS)], head_lane)

    def per_batch(bi, c):
        r0 = pl.multiple_of(c * n, n)
        rows = pl.ds(r0, n)
        trows = pl.ds(pl.multiple_of(base + r0, n), n)
        cos = cos_s[trows, :]
        sin = sin_s[trows, :]
        q = p_ref[bi, rows, 0:w] * cos + p_ref[bi, rows, w:2 * w] * sin
        k = p_ref[bi, rows, 2 * w:3 * w] * cos + p_ref[bi, rows, 3 * w:4 * w] * sin
        v = v_ref[bi, rows, :]
        qb = q.astype(BF16)
        kb = k.astype(BF16)
        qs = _dot(qb, state_s[bi].astype(BF16))
        qk = _dot_nt(jnp.concatenate([qb * hmask[h] for h in range(N_HEADS)], axis=0), kb)
        yield
        probs = [(qk[h * n:(h + 1) * n, :] * decays[h]).astype(BF16) for h in range(N_HEADS)]
        pv = _dot(jnp.concatenate(probs, axis=1),
                  jnp.concatenate([v * hmask[h] for h in range(N_HEADS)], axis=0))
        kv = _dot_tn((k * w_end).astype(BF16), v)
        yield
        y = qs * q_scale + pv
        state_s[bi] = state_s[bi] * a_chunk + jnp.where(bd, kv, 0.0)
        msq = _head_mean_sq(y, gmat)
        yield
        yn = y * lax.rsqrt(msq + RMS_EPS) * ng_ref[...]
        y_ref[bi, rows, :] = (_silu(p_ref[bi, rows, 4 * w:5 * w]) * yn).astype(BF16)

    def chunk(c, _):
        _lockstep(per_batch(bi, c) for bi in range(nb))
        return 0

    lax.fori_loop(0, ts // n, chunk, 0)


def _mixer_specs(widths_and_blocks):
    return [pl.BlockSpec((MIX_BATCH, MIX_ROWS, width), lambda bb, ss, blk=blk: (bb, ss, blk))
            for width, blk in widths_and_blocks]


def _mixer_const(shape):
    return pl.BlockSpec(shape, lambda bb, ss: (0,) * len(shape))


def _ret(pr, pb, lg_row, inv_row, sgn_row, norm_gs, layer):
    b, s, _ = pr.shape
    w = GROUP_WIDTH
    return pl.pallas_call(
        _ret_body,
        grid=(b // MIX_BATCH, s // MIX_ROWS),
        in_specs=_mixer_specs([(N_PR, 0), (w, 2)]) + [
            _mixer_const((1, LANES)), _mixer_const((1, w)), _mixer_const((1, w)),
            _layer_spec(norm_gs, layer)],
        out_specs=_mixer_specs([(w, 0)])[0],
        out_shape=jax.ShapeDtypeStruct((b, s, w), BF16),
        scratch_shapes=[pltpu.VMEM((s, w), F32), pltpu.VMEM((s, w), F32),
                        pltpu.VMEM((MIX_BATCH, w, w), F32)],
        compiler_params=_cparams(2),
        name="retention",
    )(pr, pb, lg_row, inv_row, sgn_row, norm_gs)


def _ssd_body(p_ref, gr_ref, cw_ref, cb_ref, bcol_ref, acol_ref, dskip_ref, ng_ref, y_ref,
              state_s, tail_s, x_s, q_s, k_s, v_s, vw_s, y_s, crow_s, ccol_s, ctmat_s, qscale_s,
              dtl_s, wend_s):
    nb, ts = p_ref.shape[0], p_ref.shape[1]
    n = CHUNK
    w = GROUP_WIDTH
    causal, _ = _tri(n)
    head_lane = _head_of_lane()
    hmask = _head_masks(head_lane)
    gw = SSD_GROUPS * HEAD_DIM
    hpg = N_HEADS // SSD_GROUPS
    group_lane = lax.broadcasted_iota(jnp.int32, (1, gw), 1) // HEAD_DIM
    gmask = [jnp.where(group_lane == g, 1.0, 0.0).astype(BF16) for g in range(SSD_GROUPS)]
    state_mask = (lax.broadcasted_iota(jnp.int32, (gw, w), 0) // HEAD_DIM
                  == lax.broadcasted_iota(jnp.int32, (gw, w), 1) // (HEAD_DIM * hpg))

    @pl.when(pl.program_id(1) == 0)
    def _():
        state_s[...] = jnp.zeros((nb, gw, w), F32)
        tail_s[...] = jnp.zeros((nb, 8, 2 * w), F32)

    batch = range(nb)
    n_chunks = ts // n
    half = G_DT - 8
    sel_row = lax.broadcasted_iota(jnp.int32, (GATE_ROWS, 1), 0) - half
    sel_chunk = jnp.where(
        sel_row == lax.broadcasted_iota(jnp.int32, (1, N_HEADS * n), 1) // n, 1.0, 0.0).astype(BF16)
    sel_head = jnp.where(sel_row == head_lane, 1.0, 0.0).astype(BF16)
    r_i = lax.broadcasted_iota(jnp.int32, (n, 2 * n), 0)
    c_i = lax.broadcasted_iota(jnp.int32, (n, 2 * n), 1)
    scan_mat = jnp.where(((c_i < n) & (r_i <= c_i)) | ((c_i >= n) & (r_i > c_i - n)), 1.0, 0.0
                         ).astype(BF16)
    dt16s = [_softplus(gr_ref[bi] + bcol_ref[:, 0:1]) for bi in batch]
    la_stack = jnp.concatenate([(dt16s[bi] * acol_ref[:, 0:1])[:, c * n:(c + 1) * n]
                                for bi in batch for c in range(n_chunks)], axis=0)
    scans = _cumsum_lanes(la_stack, scan_mat)

    def unstack(bi, lo):
        return jnp.concatenate(
            [scans[(bi * n_chunks + c) * GATE_ROWS + 8:(bi * n_chunks + c + 1) * GATE_ROWS, lo:lo + n]
             for c in range(n_chunks)], axis=1)

    cum8s = [unstack(bi, 0) for bi in batch]
    rev8s = [unstack(bi, n) for bi in batch]
    dt8s = [dt16[8:16, :] for dt16 in dt16s]
    ct_mats = [_rows_to_lanes(c8, sel_chunk) for c8 in cum8s]
    q_scales = [jnp.exp(_rows_to_lanes(c8, sel_head)) for c8 in cum8s]
    dt_lanes = [_rows_to_lanes(d8, sel_head) for d8 in dt8s]
    w_ends = [jnp.exp(_rows_to_lanes(r8, sel_head)) for r8 in rev8s]
    for bi in batch:
        crow_s[bi] = cum8s[bi]
        ccol_s[bi] = cum8s[bi].T
        for h in range(N_HEADS):
            ctmat_s[bi, h] = ct_mats[bi][:, h * n:(h + 1) * n]
        qscale_s[bi] = q_scales[bi]
        dtl_s[bi] = dt_lanes[bi]
        wend_s[bi] = w_ends[bi]

    def conv_step(c, _):
        rows = pl.ds(pl.multiple_of(c * n, n), n)
        for bi in batch:
            pre = p_ref[bi, rows, w:3 * w]
            xbc = _causal_conv_silu(pre, tail_s[bi], cw_ref, cb_ref)
            tail_s[bi] = pre[n - 8:n, :]
            x = xbc[:, 0:w]
            x_s[bi, rows, :] = x
            k_s[bi, rows, :] = xbc[:, w:w + gw].astype(BF16)
            q_s[bi, rows, :] = xbc[:, w + gw:w + 2 * gw].astype(BF16)
            v = x * dtl_s[bi, rows, :]
            v_s[bi, rows, :] = v.astype(BF16)
            vw_s[bi, rows, :] = (v * wend_s[bi, rows, :]).astype(BF16)
        return 0

    lax.fori_loop(0, n_chunks, conv_step, 0)

    def per_batch(bi, c):
        rows = pl.ds(pl.multiple_of(c * n, n), n)
        qb = q_s[bi, rows, :]
        kb = k_s[bi, rows, :]
        vb = v_s[bi, rows, :]
        qs = _dot(qb, state_s[bi].astype(BF16))
        qk = _dot_nt(jnp.concatenate([qb * gmask[g] for g in range(SSD_GROUPS)], axis=0), kb)
        kv = _dot_tn(kb, vw_s[bi, rows, :])
        yield
        c_rows = crow_s[bi, :, rows]
        probs = [(qk[(h // hpg) * n:(h // hpg + 1) * n, :]
                  * jnp.where(causal, jnp.exp(ctmat_s[bi, h, rows, :]
                                              - c_rows[half + h:half + h + 1, :]), 0.0)
                  ).astype(BF16) for h in range(N_HEADS)]
        pv = _dot(jnp.concatenate(probs, axis=1),
                  jnp.concatenate([vb * hmask[h] for h in range(N_HEADS)], axis=0))
        yield
        y_s[bi, rows, :] = qs * qscale_s[bi, rows, :] + pv
        c_end = ccol_s[bi, pl.ds(pl.multiple_of(c * n, n) + n - 8, 8), :][7:8, :]
        a_chunk = _expand([jnp.exp(c_end[:, half + h:half + h + 1]) for h in range(N_HEADS)],
                          head_lane)
        state_s[bi] = state_s[bi] * a_chunk + jnp.where(state_mask, kv, 0.0)

    def chunk(c, _):
        _lockstep(per_batch(bi, c) for bi in batch)
        return 0

    lax.fori_loop(0, n_chunks, chunk, 0)

    def norm_step(c, _):
        rows = pl.ds(pl.multiple_of(c * n, n), n)
        for bi in batch:
            hs = (y_s[bi, rows, :] + dskip_ref[...] * x_s[bi, rows, :]) * _silu(p_ref[bi, rows, 0:w])
            yn = hs * lax.rsqrt(jnp.mean(hs * hs, axis=-1, keepdims=True) + RMS_EPS) * ng_ref[...]
            y_ref[bi, rows, :] = yn.astype(BF16)
        return 0

    lax.fori_loop(0, n_chunks, norm_step, 0)


def _gate_row_spec():
    return pl.BlockSpec((MIX_BATCH, GATE_ROWS, MIX_ROWS), lambda bb, ss: (bb, 0, ss))


def _ssd(ps, gr, conv_ws, conv_bs, bias_cols, a_cols, d_rows, norm_gs, layer):
    b, s, _ = ps.shape
    w = GROUP_WIDTH
    gw = SSD_GROUPS * HEAD_DIM
    nb = MIX_BATCH
    params = (conv_ws, conv_bs, bias_cols, a_cols, d_rows, norm_gs)
    return pl.pallas_call(
        _ssd_body,
        grid=(b // MIX_BATCH, s // MIX_ROWS),
        in_specs=_mixer_specs([(N_PS, 0)]) + [_gate_row_spec()] + [
            _layer_spec(p, layer) for p in params],
        out_specs=_mixer_specs([(w, 0)])[0],
        out_shape=jax.ShapeDtypeStruct((b, s, w), BF16),
        scratch_shapes=[pltpu.VMEM((nb, gw, w), F32),
                        pltpu.VMEM((nb, 8, 2 * w), F32),
                        pltpu.VMEM((nb, MIX_ROWS, w), F32),
                        pltpu.VMEM((nb, MIX_ROWS, gw), BF16),
                        pltpu.VMEM((nb, MIX_ROWS, gw), BF16),
                        pltpu.VMEM((nb, MIX_ROWS, w), BF16),
                        pltpu.VMEM((nb, MIX_ROWS, w), BF16),
                        pltpu.VMEM((nb, MIX_ROWS, w), F32),
                        pltpu.VMEM((nb, 8, MIX_ROWS), F32),
                        pltpu.VMEM((nb, MIX_ROWS, 8), F32),
                        pltpu.VMEM((nb, N_HEADS, MIX_ROWS, CHUNK), F32),
                        pltpu.VMEM((nb, MIX_ROWS, w), F32),
                        pltpu.VMEM((nb, MIX_ROWS, w), F32),
                        pltpu.VMEM((nb, MIX_ROWS, w), F32)],
        compiler_params=_cparams(2),
        name="ssd",
    )(ps, gr, *params)


def _rows_to_lanes(x8, sel):
    parts = _split3(jnp.concatenate([x8, jnp.zeros_like(x8)], axis=0))
    return _dot_tn(jnp.concatenate(parts, axis=0), jnp.concatenate([sel, sel, sel], axis=0))


def _chunk_cummax(xs, n):
    lane_in_chunk = lax.broadcasted_iota(jnp.int32, (1, xs[0].shape[1]), 1) % n
    shift = 1
    while shift < n:
        keep = lane_in_chunk >= shift
        xs = [jnp.maximum(x, jnp.where(keep, pltpu.roll(x, shift, 1), NEG_BIG)) for x in xs]
        shift *= 2
    return xs


def _mlstm_body(p_ref, v_ref, gr_ref, cw_ref, cb_ref, bcol_ref, ng_ref, y_ref,
                state_s, m_s, tail_s, q_s, k_s, h_s, rrow_s, colb_s, colc_s, cmat_s, cexp_s,
                rexp_s):
    nb, ts = p_ref.shape[0], p_ref.shape[1]
    n = CHUNK
    w = GROUP_WIDTH
    wa = w + LANES
    causal, _ = _tri(n)
    head_lane = _head_of_lane()
    hmask = _head_masks(head_lane)
    head_lane_aug = jnp.concatenate(
        [head_lane, lax.broadcasted_iota(jnp.int32, (1, LANES), 1)], axis=1)
    gmat = _group_mean_matrix()
    bd_aug = _block_diag_mask(wa)
    lane128 = lax.broadcasted_iota(jnp.int32, (1, LANES), 1)
    ones_col = [jnp.where(lane128 == h, 1.0, 0.0).astype(BF16) for h in range(N_HEADS)]

    @pl.when(pl.program_id(1) == 0)
    def _():
        state_s[...] = jnp.zeros((nb, w, wa), F32)
        m_s[...] = jnp.zeros((nb, 1, 8), F32)
        tail_s[...] = jnp.zeros((nb, 8, 2 * w), F32)

    _, tri_u = _tri(n)
    sel_row = lax.broadcasted_iota(jnp.int32, (GATE_ROWS, 1), 0)
    sel_chunk = jnp.where(
        sel_row == lax.broadcasted_iota(jnp.int32, (1, N_HEADS * n), 1) // n, 1.0, 0.0).astype(BF16)
    sel_head = jnp.where(sel_row == head_lane, 1.0, 0.0).astype(BF16)
    batch = range(nb)
    n_chunks = ts // n
    g_blks = [gr_ref[bi] + bcol_ref[:, 0:1] for bi in batch]
    lf_blks = [_log_sigmoid(g) for g in g_blks]
    b_stack = _cumsum_lanes(
        jnp.concatenate([lf[:, c * n:(c + 1) * n] for lf in lf_blks for c in range(n_chunks)], axis=0),
        tri_u)
    b16s = [jnp.concatenate([b_stack[(bi * n_chunks + c) * GATE_ROWS:(bi * n_chunks + c + 1) * GATE_ROWS]
                             for c in range(n_chunks)], axis=1) for bi in batch]
    r16s = [pltpu.roll(g_blks[bi], G_MF - G_MI, 0) - b16s[bi] for bi in batch]
    b8s = [b16[G_MF:G_MF + 8, :] for b16 in b16s]
    r8s = [r16[G_MF:G_MF + 8, :] for r16 in r16s]
    c8s = _chunk_cummax(r8s, n)
    c_mats = [_rows_to_lanes(c8, sel_chunk) for c8 in c8s]
    c_exps = [_rows_to_lanes(c8, sel_head) for c8 in c8s]
    r_exps = [_rows_to_lanes(r8, sel_head) for r8 in r8s]
    for bi in batch:
        rrow_s[bi] = r8s[bi]
        colb_s[bi] = b8s[bi].T
        colc_s[bi] = c8s[bi].T
        for h in range(N_HEADS):
            cmat_s[bi, h] = c_mats[bi][:, h * n:(h + 1) * n]
        cexp_s[bi] = c_exps[bi]
        rexp_s[bi] = r_exps[bi]
    def conv_step(c, _):
        rows = pl.ds(pl.multiple_of(c * n, n), n)
        for bi in batch:
            pre = p_ref[bi, rows, 0:2 * w]
            qk_act = _causal_conv_silu(pre, tail_s[bi], cw_ref, cb_ref)
            tail_s[bi] = pre[n - 8:n, :]
            q_s[bi, rows, :] = (qk_act[:, 0:w] * (HEAD_DIM ** -0.5)).astype(BF16)
            k_s[bi, rows, :] = qk_act[:, w:2 * w]
        return 0

    lax.fori_loop(0, n_chunks, conv_step, 0)

    def per_batch(bi, c):
        r0 = pl.multiple_of(c * n, n)
        rows = pl.ds(r0, n)
        qb = q_s[bi, rows, :]
        k = k_s[bi, rows, :]
        kb = k.astype(BF16)
        vb = v_ref[bi, rows, :]

        qs = _dot(qb, state_s[bi].astype(BF16))
        qk = _dot_nt(jnp.concatenate([qb * hmask[h] for h in range(N_HEADS)], axis=0), kb)
        yield

        heads = range(N_HEADS)
        m8 = m_s[bi]
        m_ins = [m8[:, h:h + 1] for h in heads]
        r_rows = rrow_s[bi, :, rows]
        col_b = colb_s[bi, rows, :]
        col_c = colc_s[bi, rows, :]
        c_end8 = col_c[n - 1:n, :]
        z_end8 = jnp.maximum(m8, c_end8)
        probs = [(jnp.where(causal,
                            jnp.exp(r_rows[h:h + 1, :] - jnp.maximum(m_ins[h], cmat_s[bi, h, rows, :])),
                            0.0) * qk[h * n:(h + 1) * n, :]).astype(BF16) for h in heads]
        v_rows = jnp.concatenate(
            [jnp.concatenate([vb * hmask[h], jnp.broadcast_to(ones_col[h], (n, LANES))], axis=1)
             for h in heads], axis=0)
        pv = _dot(jnp.concatenate(probs, axis=1), v_rows)
        c_ends = [c_end8[:, h:h + 1] for h in heads]
        kw = (k * jnp.exp(rexp_s[bi, rows, :] - _expand(c_ends, head_lane))).astype(BF16)
        v_aug = jnp.concatenate([vb, jnp.ones((n, LANES), BF16)], axis=1)
        kv = _dot_tn(kw, v_aug)
        yield
        m_lanes = _expand(m_ins, head_lane)
        a_inter = jnp.exp(m_lanes - jnp.maximum(m_lanes, cexp_s[bi, rows, :]))
        num = pv[:, 0:w] + a_inter * qs[:, 0:w]
        z8 = jnp.maximum(m8, col_c)
        den8 = pv[:, w:w + 8] + jnp.exp(m8 - z8) * qs[:, w:w + 8]
        dens8 = jnp.maximum(jnp.abs(den8), jnp.exp(-(col_b + z8)))
        h_s[bi, rows, :] = num / _expand([dens8[:, h:h + 1] for h in heads], head_lane)
        a_old8 = jnp.exp(m8 - z_end8)
        a_loc8 = jnp.exp(c_end8 - z_end8)
        state_s[bi] = (state_s[bi] * _expand([a_old8[:, h:h + 1] for h in heads], head_lane_aug)
                       + jnp.where(bd_aug, kv, 0.0)
                       * _expand([a_loc8[:, h:h + 1] for h in heads], head_lane_aug))
        m_s[bi] = col_b[n - 1:n, :] + z_end8

    def chunk(c, _):
        _lockstep(per_batch(bi, c) for bi in range(nb))
        return 0

    lax.fori_loop(0, ts // n, chunk, 0)

    def norm_step(c, _):
        rows = pl.ds(pl.multiple_of(c * n, n), n)
        hvals = [h_s[bi, rows, :] for bi in batch]
        msqs = [_head_mean_sq(hv, gmat) for hv in hvals]
        for bi in batch:
            yn = hvals[bi] * lax.rsqrt(msqs[bi] + RMS_EPS) * ng_ref[...]
            y_ref[bi, rows, :] = (_sigmoid(p_ref[bi, rows, 2 * w:3 * w]) * yn).astype(BF16)
        return 0

    lax.fori_loop(0, n_chunks, norm_step, 0)


def _mlstm(pm, pb, gr, conv_ws, conv_bs, bias_cols, norm_gs, layer):
    b, s, _ = pm.shape
    w = GROUP_WIDTH
    params = (conv_ws, conv_bs, bias_cols, norm_gs)
    return pl.pallas_call(
        _mlstm_body,
        grid=(b // MIX_BATCH, s // MIX_ROWS),
        in_specs=_mixer_specs([(N_PM, 0), (w, 1)]) + [_gate_row_spec()] + [
            _layer_spec(p, layer) for p in params],
        out_specs=_mixer_specs([(w, 0)])[0],
        out_shape=jax.ShapeDtypeStruct((b, s, w), BF16),
        scratch_shapes=[pltpu.VMEM((MIX_BATCH, w, w + LANES), F32),
                        pltpu.VMEM((MIX_BATCH, 1, 8), F32),
                        pltpu.VMEM((MIX_BATCH, 8, 2 * w), F32),
                        pltpu.VMEM((MIX_BATCH, MIX_ROWS, w), BF16),
                        pltpu.VMEM((MIX_BATCH, MIX_ROWS, w), F32),
                        pltpu.VMEM((MIX_BATCH, MIX_ROWS, w), F32),
                        pltpu.VMEM((MIX_BATCH, 8, MIX_ROWS), F32),
                        pltpu.VMEM((MIX_BATCH, MIX_ROWS, 8), F32),
                        pltpu.VMEM((MIX_BATCH, MIX_ROWS, 8), F32),
                        pltpu.VMEM((MIX_BATCH, N_HEADS, MIX_ROWS, CHUNK), F32),
                        pltpu.VMEM((MIX_BATCH, MIX_ROWS, w), F32),
                        pltpu.VMEM((MIX_BATCH, MIX_ROWS, w), F32)],
        compiler_params=_cparams(2),
        name="mlstm",
    )(pm, pb, gr, *params)


def _half_swap(t):
    lead = t.shape[:-1]
    t = t.reshape(lead + (N_HEADS, 2, HEAD_DIM // 2))
    return t[..., ::-1, :].reshape(lead + (GROUP_WIDTH,))


def _inproj_weights(w_in):
    w, h, hd = GROUP_WIDTH, N_HEADS, HEAD_DIM
    w_f32 = w_in
    w_in = w_in.astype(BF16)
    qk_scale = hd ** -0.5
    fox0 = 0
    ml0 = fox0 + 3 * w + h
    ret0 = ml0 + 4 * w + 2 * h
    ssd0 = ret0 + 4 * w
    bc = 2 * SSD_GROUPS * hd

    def col(start, width, src=None):
        return (w_in if src is None else src)[:, :, start:start + width]

    rq, rk = col(ret0, w), col(ret0 + w, w) * qk_scale
    gates = jnp.concatenate([col(fox0 + 3 * w, h, w_f32), col(ml0 + 3 * w, 2 * h, w_f32),
                             col(ssd0 + 2 * w + bc, h, w_f32)], axis=2)
    w_t = jnp.concatenate([col(fox0, w, w_f32) * qk_scale, col(fox0 + 2 * w, w, w_f32), gates],
                          axis=2)
    w_all = jnp.concatenate([
        col(fox0 + w, w), col(ml0 + 2 * w, w), col(ret0 + 2 * w, w),
        col(ml0, 2 * w), col(ml0 + 3 * w + 2 * h, w),
        rq, _half_swap(rq), rk, _half_swap(rk), col(ret0 + 3 * w, w),
        col(ssd0, w), col(ssd0 + w, w + bc)], axis=2)
    return w_all, jnp.swapaxes(w_t, 1, 2).astype(BF16)


def _lane_pad(v, width=LANES):
    return jnp.zeros((width,), F32).at[:v.shape[0]].set(v)


def kernel(x, c, ada_w, ada_b, norm_g, ffn1_w13, ffn1_w2, ffn2_w13, ffn2_w2, w_in, w_out, fox_fb,
           mlstm_conv_w, mlstm_conv_b, mlstm_ib, mlstm_fb, mlstm_norm_g, ret_norm_g, ssd_conv_w,
           ssd_conv_b, ssd_dt_bias, ssd_A_log, ssd_D, ssd_norm_g, final_g):
    n_layers = ada_w.shape[0]
    b, s, d = x.shape
    hd = HEAD_DIM

    cond = _adaln(c, ada_w, ada_b).reshape(n_layers * b * N_MOD, 1, d)

    inv = 1.0 / (ROPE_BASE ** (jnp.arange(0, hd, 2, dtype=F32) / hd))
    inv_row = jnp.tile(inv, 2 * N_HEADS).reshape(1, GROUP_WIDTH)
    sgn_row = jnp.tile(jnp.concatenate([-jnp.ones(hd // 2, F32), jnp.ones(hd // 2, F32)]),
                       N_HEADS).reshape(1, GROUP_WIDTH)
    log_gamma = jnp.log(1.0 - 2.0 ** (-RET_DECAY_OFFSET - jnp.arange(N_HEADS, dtype=F32)))
    lg_row = _lane_pad(log_gamma).reshape(1, LANES)

    w13_a, w2_a = ffn1_w13.astype(BF16), ffn1_w2.astype(BF16)
    w13_b, w2_b = ffn2_w13.astype(BF16), ffn2_w2.astype(BF16)
    w_out_b = w_out.astype(BF16)
    w_all, w_t = _inproj_weights(w_in)

    norms = norm_g.reshape(n_layers * 3, 1, d)
    final_row = final_g.reshape(1, d)
    gate_bias = jnp.concatenate([fox_fb, mlstm_ib, mlstm_fb, ssd_dt_bias], axis=1)
    bias_cols = jnp.broadcast_to(gate_bias[:, :, None], (n_layers, GATE_ROWS, LANES))
    a_rows = jnp.zeros((n_layers, GATE_ROWS), F32).at[:, G_DT:G_DT + N_HEADS].set(
        -jnp.exp(ssd_A_log.astype(F32)))
    a_cols = jnp.broadcast_to(a_rows[:, :, None], (n_layers, GATE_ROWS, LANES))
    d_rows = jnp.repeat(ssd_D, hd, axis=1).reshape(n_layers, 1, GROUP_WIDTH)
    mlstm_cb = mlstm_conv_b.reshape(n_layers, 1, -1)
    ssd_cb = ssd_conv_b.reshape(n_layers, 1, -1)
    mlstm_ng = mlstm_norm_g.reshape(n_layers, 1, GROUP_WIDTH)
    ret_ng = ret_norm_g.reshape(n_layers, 1, GROUP_WIDTH)
    ssd_ng = ssd_norm_g.reshape(n_layers, 1, GROUP_WIDTH)

    for l in range(n_layers):
        def cond_base(bb, l=l):
            return (l * b + bb) * N_MOD

        x = _ffn(x, cond, cond_base, 0, norms, 3 * l, w13_a, w2_a, l, final_row, False)
        pb, pm, pr, ps, pt, gr = _inproj(x, cond, cond_base, norms, 3 * l + 1, w_all, w_t, l)

        y_fox = _fox(pb, pt, gr, bias_cols, l)
        y_mlstm = _mlstm(pm, pb, gr, mlstm_conv_w, mlstm_cb, bias_cols, mlstm_ng, l)
        y_ret = _ret(pr, pb, lg_row, inv_row, sgn_row, ret_ng, l)
        y_ssd = _ssd(ps, gr, ssd_conv_w, ssd_cb, bias_cols, a_cols, d_rows, ssd_ng, l)

        x = _ffn(x, cond, cond_base, 6, norms, 3 * l + 2, w13_b, w2_b, l, final_row,
                 l == n_layers - 1, mixer_ys=(y_fox, y_mlstm, y_ret, y_ssd), w_out=w_out_b)
    return x
```
